```python
import math
import jax
import jax.numpy as jnp
from jax import lax
import numpy as np

D_MODEL = 1024
BATCH = 16
SEQ = 2048
DEPTH = 4

CTX_LEN = 256
GRID_W = 64
EPS = 1e-6

D_SSM = 512
SSM_GROUP = 16
N_SSM_GROUPS = D_SSM // SSM_GROUP
SSM_STATE = 64
D_MLSTM = 512
N_MLSTM_HEADS = 4
MLSTM_HEAD = D_MLSTM // N_MLSTM_HEADS
MLSTM_CHUNK = 128
QK_CONV = 3
N_DIFF_HEADS = 4
DIFF_QK_HEAD = 64
DIFF_V_HEAD = 2 * DIFF_QK_HEAD
D_DIFF = N_DIFF_HEADS * DIFF_V_HEAD
D_DIFF_QK = N_DIFF_HEADS * 2 * DIFF_QK_HEAD
Q_BLOCK = 128
ROPE_BASE = 10000.0

N_BRANCH = 3
IN_SPLITS = (D_SSM, D_SSM, 2 * D_MLSTM, D_MLSTM, D_MLSTM, D_MLSTM, 4 * N_MLSTM_HEADS,
             D_DIFF_QK, D_DIFF_QK, D_DIFF, D_DIFF, N_BRANCH * D_MODEL)
D_IN = 2 * D_SSM + 5 * D_MLSTM + 4 * N_MLSTM_HEADS + 2 * D_DIFF_QK + 2 * D_DIFF + N_BRANCH * D_MODEL

kernel_name = 'hybrid_s5_mlstm_diffattn_prefix_block'


def rmsnorm(x, g):
    xf = x.astype(jnp.float32)
    y = xf * lax.rsqrt(jnp.mean(xf * xf, axis=-1, keepdims=True) + EPS)
    return (y * g.astype(jnp.float32)).astype(x.dtype)


def split_cols(p):
    out, off = [], 0
    for w in IN_SPLITS:
        out.append(p[..., off:off + w])
        off += w
    return out


def axial_rope_tables(n_tok):
    rows = n_tok // GRID_W
    row = jnp.repeat(jnp.arange(rows), GRID_W).astype(jnp.float32)
    col = jnp.tile(jnp.arange(GRID_W), rows).astype(jnp.float32)
    half = DIFF_QK_HEAD // 2
    inv = jnp.power(ROPE_BASE, -jnp.arange(0, half, 2, dtype=jnp.float32) / half)
    ang_r = row[:, None] * inv
    ang_c = col[:, None] * inv
    shp = (n_tok, 1, 1, half // 2)
    return (jnp.cos(ang_r).reshape(shp), jnp.sin(ang_r).reshape(shp),
            jnp.cos(ang_c).reshape(shp), jnp.sin(ang_c).reshape(shp))


def rope_1d(x, cos, sin):
    x1, x2 = jnp.split(x, 2, axis=-1)
    return jnp.concatenate([x1 * cos - x2 * sin, x2 * cos + x1 * sin], axis=-1)


def apply_axial_rope(x, rope):
    cos_r, sin_r, cos_c, sin_c = rope
    xr, xc = jnp.split(x, 2, axis=-1)
    y = jnp.concatenate([rope_1d(xr, cos_r, sin_r), rope_1d(xc, cos_c, sin_c)], axis=-1)
    return y.astype(x.dtype)


def s5_discretize(lam_re, lam_im, log_step, b_re, b_im):
    f32 = jnp.float32
    lre = jnp.minimum(lam_re.astype(f32), -1e-4)
    lim = lam_im.astype(f32)
    dt = jnp.exp(log_step.astype(f32))[:, None]
    ld_re, ld_im = lre * dt, lim * dt
    mag = jnp.exp(ld_re)
    ab_re, ab_im = mag * jnp.cos(ld_im), mag * jnp.sin(ld_im)
    den = lre * lre + lim * lim
    num_re, num_im = ab_re - 1.0, ab_im
    coef_re = (num_re * lre + num_im * lim) / den
    coef_im = (num_im * lre - num_re * lim) / den
    bre, bim = b_re.astype(f32), b_im.astype(f32)
    bb_re = coef_re[..., None] * bre - coef_im[..., None] * bim
    bb_im = coef_re[..., None] * bim + coef_im[..., None] * bre
    return ld_re, ld_im, ab_re, ab_im, bb_re, bb_im


def _cmul(ar, ai, br, bi):
    return ar * br - ai * bi, ar * bi + ai * br


def _ssm_combine(e1, e2):
    a1r, a1i, b1r, b1i = e1
    a2r, a2i, b2r, b2i = e2
    ar, ai = _cmul(a2r, a2i, a1r, a1i)
    br, bi = _cmul(a2r, a2i, b1r, b1i)
    return ar, ai, br + b2r, bi + b2i


def s5_states(u, disc, h0):
    ld_re, ld_im, ab_re, ab_im, bb_re, bb_im = disc
    bu_re = jnp.einsum('blgn,gpn->blgp', u, bb_re)
    bu_im = jnp.einsum('blgn,gpn->blgp', u, bb_im)
    a_re = jnp.broadcast_to(ab_re, bu_re.shape)
    a_im = jnp.broadcast_to(ab_im, bu_im.shape)
    _, _, x_re, x_im = lax.associative_scan(_ssm_combine, (a_re, a_im, bu_re, bu_im), axis=1)
    if h0 is not None:
        t = jnp.arange(1, u.shape[1] + 1, dtype=jnp.float32)[:, None, None]
        mag = jnp.exp(t * ld_re)
        ph = t * ld_im
        add_re, add_im = _cmul(mag * jnp.cos(ph), mag * jnp.sin(ph), h0[0][:, None], h0[1][:, None])
        x_re = x_re + add_re
        x_im = x_im + add_im
    return x_re, x_im


def s5_readout(x_re, x_im, c_re, c_im):
    return jnp.einsum('blgp,gnp->blgn', x_re, c_re) - jnp.einsum('blgp,gnp->blgn', x_im, c_im)


def s5_branch(u_c, u_l, lam_re, lam_im, log_step, b_re, b_im, c_re, c_im, d_skip, glu_w, glu_b, with_ctx):
    f32 = jnp.float32
    G, N = N_SSM_GROUPS, SSM_GROUP
    B = u_l.shape[0]
    uc = u_c.astype(f32).reshape(B, -1, G, N)
    ul = u_l.astype(f32).reshape(B, -1, G, N)
    dg = d_skip.astype(f32).reshape(G, N)
    y_l = dg * ul
    y_c = dg * uc
    for d in range(2):
        disc = s5_discretize(lam_re[d], lam_im[d], log_step[d], b_re[d], b_im[d])
        cr, ci = c_re[d].astype(f32), c_im[d].astype(f32)
        uc_d = uc if d == 0 else jnp.flip(uc, 1)
        ul_d = ul if d == 0 else jnp.flip(ul, 1)
        xc_re, xc_im = s5_states(uc_d, disc, None)
        xl_re, xl_im = s5_states(ul_d, disc, (xc_re[:, -1], xc_im[:, -1]))
        yl_d = s5_readout(xl_re, xl_im, cr, ci)
        y_l = y_l + (yl_d if d == 0 else jnp.flip(yl_d, 1))
        if with_ctx:
            yc_d = s5_readout(xc_re, xc_im, cr, ci)
            y_c = y_c + (yc_d if d == 0 else jnp.flip(yc_d, 1))

    def glu(y):
        y = jax.nn.gelu(y.reshape(B, -1, D_SSM))
        a, g = jnp.split(y @ glu_w.astype(f32) + glu_b.astype(f32), 2, axis=-1)
        return (a * jax.nn.sigmoid(g)).astype(u_l.dtype)

    return (glu(y_c) if with_ctx else None), glu(y_l)


def dwconv_centred(x, w, b):
    pad = w.shape[0] // 2
    y = lax.conv_general_dilated(x, w[:, None, :].astype(x.dtype), (1,), [(pad, pad)],
                                 dimension_numbers=('NWC', 'WIO', 'NWC'),
                                 feature_group_count=x.shape[-1])
    return y + b.astype(x.dtype)


def mlstm_chunkwise(q, k, v, i_pre, f_pre, state0):
    f32 = jnp.float32
    B, L, H, dh = q.shape
    T = MLSTM_CHUNK
    nc = L // T
    q = q.astype(f32).reshape(B, nc, T, H, dh)
    k = (k.astype(f32) * dh ** -0.5).reshape(B, nc, T, H, dh)
    v = v.astype(f32).reshape(B, nc, T, H, dh)
    ig = i_pre.astype(f32).reshape(B, nc, T, H)
    F = jnp.cumsum(jax.nn.log_sigmoid(f_pre.astype(f32)).reshape(B, nc, T, H), axis=2)
    F_end = F[:, :, -1]
    w_end = F_end[:, :, None] - F + ig
    m_loc = jnp.max(w_end, axis=2)
    e_end = jnp.exp(w_end - m_loc[:, :, None])
    dC = jnp.einsum('bcth,bcthd,bcthe->bchde', e_end, k, v)
    dn = jnp.einsum('bcth,bcthd->bchd', e_end, k)

    def step(carry, inp):
        C, n, m = carry
        dC_c, dn_c, m_c, fe = inp
        m_new = jnp.maximum(fe + m, m_c)
        a = jnp.exp(fe + m - m_new)
        b = jnp.exp(m_c - m_new)
        C_new = a[..., None, None] * C + b[..., None, None] * dC_c
        n_new = a[..., None] * n + b[..., None] * dn_c
        return (C_new, n_new, m_new), (C, n, m)

    xs = tuple(jnp.moveaxis(t, 1, 0) for t in (dC, dn, m_loc, F_end))
    state0 = tuple(s.astype(f32) for s in state0)
    final, starts = lax.scan(step, state0, xs)
    C0, n0, m0 = (jnp.moveaxis(t, 0, 1) for t in starts)

    tri = jnp.tril(jnp.ones((T, T), dtype=bool))
    log_d = F[:, :, :, None, :] - F[:, :, None, :, :] + ig[:, :, None, :, :]
    log_d = jnp.where(tri[None, None, :, :, None], log_d, -jnp.inf)
    log_inter = F + m0[:, :, None]
    m_t = jnp.maximum(jnp.max(log_d, axis=3), log_inter)
    s_qk = jnp.einsum('bcthd,bcshd->bctsh', q, k) * jnp.exp(log_d - m_t[:, :, :, None])
    e_inter = jnp.exp(log_inter - m_t)
    num = (jnp.einsum('bctsh,bcshe->bcthe', s_qk, v)
           + e_inter[..., None] * jnp.einsum('bcthd,bchde->bcthe', q, C0))
    den = jnp.sum(s_qk, axis=3) + e_inter * jnp.einsum('bcthd,bchd->bcth', q, n0)
    h = num / jnp.maximum(jnp.abs(den), jnp.exp(-m_t))[..., None]
    return h.reshape(B, L, H, dh), final


def mlstm_branch(in_c, in_l, conv_w, conv_b, gate_b, norm_g, with_ctx):
    H, dh = N_MLSTM_HEADS, MLSTM_HEAD

    def prep(qk, v, g):
        B, L, _ = qk.shape
        qk = jax.nn.silu(dwconv_centred(qk, conv_w, conv_b))
        q, k = jnp.split(qk, 2, axis=-1)
        gates = g.reshape(B, L, 4, H) + gate_b.astype(g.dtype)
        return (q.reshape(B, L, H, dh), k.reshape(B, L, H, dh), v.reshape(B, L, H, dh), gates)

    def orient(seq, d):
        q, k, v, g = seq
        args = (q, k, v, g[:, :, 2 * d], g[:, :, 2 * d + 1])
        return tuple(jnp.flip(a, 1) for a in args) if d == 1 else args

    qk_c, v_c, o_c, g_c = in_c
    qk_l, v_l, o_l, g_l = in_l
    seq_c = prep(qk_c, v_c, g_c)
    seq_l = prep(qk_l, v_l, g_l)
    B = qk_l.shape[0]
    zero_state = (jnp.zeros((B, H, dh, dh), jnp.float32), jnp.zeros((B, H, dh), jnp.float32),
                  jnp.zeros((B, H), jnp.float32))
    h_c = 0.0
    h_l = 0.0
    for d in range(2):
        hc_d, st = mlstm_chunkwise(*orient(seq_c, d), zero_state)
        hl_d, _ = mlstm_chunkwise(*orient(seq_l, d), st)
        h_l = h_l + (hl_d if d == 0 else jnp.flip(hl_d, 1))
        h_c = h_c + (hc_d if d == 0 else jnp.flip(hc_d, 1))
    g_heads = norm_g.reshape(H, dh)

    def finish(h, o):
        h = rmsnorm(h, g_heads).reshape(h.shape[0], h.shape[1], D_MLSTM)
        return (h * jax.nn.sigmoid(o.astype(jnp.float32))).astype(o.dtype)

    return (finish(h_c, o_c) if with_ctx else None), finish(h_l, o_l)


def _diff_attend(q, k, v, lam):
    s = jnp.einsum('bqhcd,bkhcd->bchqk', q, k).astype(jnp.float32) * DIFF_QK_HEAD ** -0.5
    p = jax.nn.softmax(s, axis=-1)
    w = p[:, 0] - lam * p[:, 1]
    return jnp.einsum('bhqk,bkhd->bqhd', w.astype(v.dtype), v)


def diff_attn_branch(in_c, in_l, qn_g, kn_g, lam_p, subln_g, rope, layer_idx, with_ctx):
    H, dk, dv = N_DIFF_HEADS, DIFF_QK_HEAD, DIFF_V_HEAD

    def heads(q, k, v):
        B, L, _ = q.shape
        q = rmsnorm(q.reshape(B, L, H, 2, dk), qn_g)
        k = rmsnorm(k.reshape(B, L, H, 2, dk), kn_g)
        return q, k, v.reshape(B, L, H, dv)

    qc, kc, vc = heads(*in_c)
    ql, kl, vl = heads(*in_l)
    ql = apply_axial_rope(ql, rope)
    kl = apply_axial_rope(kl, rope)
    lam_init = 0.8 - 0.6 * math.exp(-0.3 * layer_idx)
    lp = lam_p.astype(jnp.float32)
    lam = jnp.exp(jnp.sum(lp[0] * lp[1])) - jnp.exp(jnp.sum(lp[2] * lp[3])) + lam_init

    B, L = ql.shape[0], ql.shape[1]
    keys = jnp.concatenate([kl, kc], axis=1)
    vals = jnp.concatenate([vl, vc], axis=1)
    nb = L // Q_BLOCK
    qb = jnp.moveaxis(ql.reshape(B, nb, Q_BLOCK, H, 2, dk), 1, 0)
    ol = lax.map(lambda qq: _diff_attend(qq, keys, vals, lam), qb)
    ol = jnp.moveaxis(ol, 0, 1).reshape(B, L, H, dv)

    def post(o):
        return (rmsnorm(o, subln_g) * (1.0 - lam_init)).reshape(o.shape[0], o.shape[1], D_DIFF)

    out_c = post(_diff_attend(qc, kc, vc, lam)) if with_ctx else None
    return out_c, post(ol)


def merge_branches(ys, zs, gate_logits, w_a, w_b, w_c, w_o):
    gates = jnp.split(jax.nn.sigmoid(gate_logits), N_BRANCH, axis=-1)
    m = 0.0
    for g, y, z, w in zip(gates, ys, zs, (w_a, w_b, w_c)):
        m = m + g * ((y * jax.nn.silu(z)) @ w)
    return m @ w_o


def setup_inputs(seed: int = 0) -> dict:
    key = jax.random.key(seed)
    ks = iter(jax.random.split(key, 40))
    f32 = jnp.float32

    def nrm(shape, s):
        return s * jax.random.normal(next(ks), shape, f32)

    L, G, P, N, H = DEPTH, N_SSM_GROUPS, SSM_STATE, SSM_GROUP, N_MLSTM_HEADS
    x = nrm((BATCH, SEQ, D_MODEL), 1.0)
    c = nrm((BATCH, D_MODEL), 1.0)
    ctx = nrm((BATCH, CTX_LEN, D_MODEL), 1.0)
    c_ctx = nrm((D_MODEL,), 1.0)
    norm_g = 1.0 + nrm((L, D_MODEL), 0.02)
    ada_w = nrm((L, D_MODEL, 3 * D_MODEL), 0.5 * D_MODEL ** -0.5)
    ada_b = nrm((L, 3 * D_MODEL), 0.02)
    w_in = nrm((L, D_MODEL, D_IN), D_MODEL ** -0.5)
    ssm_lam_re = -0.5 + nrm((L, 2, G, P), 0.01)
    ssm_lam_im = jnp.pi * jnp.arange(P, dtype=f32) + nrm((L, 2, G, P), 0.01)
    ssm_log_step = jax.random.uniform(next(ks), (L, 2, G), f32, math.log(1e-3), math.log(1e-1))
    ssm_b_re = nrm((L, 2, G, P, N), (2 * N) ** -0.5)
    ssm_b_im = nrm((L, 2, G, P, N), (2 * N) ** -0.5)
    ssm_c_re = nrm((L, 2, G, N, P), P ** -0.5)
    ssm_c_im = nrm((L, 2, G, N, P), P ** -0.5)
    ssm_d = nrm((L, D_SSM), 1.0)
    ssm_glu_w = nrm((L, D_SSM, 2 * D_SSM), D_SSM ** -0.5)
    ssm_glu_b = nrm((L, 2 * D_SSM), 0.02)
    w_ssm_out = nrm((L, D_SSM, D_MODEL), D_SSM ** -0.5)
    ml_conv_w = nrm((L, QK_CONV, 2 * D_MLSTM), QK_CONV ** -0.5)
    ml_conv_b = nrm((L, 2 * D_MLSTM), 0.02)
    fb = jnp.linspace(3.0, 6.0, H, dtype=f32)
    zb = jnp.zeros((H,), f32)
    ml_gate_b = jnp.stack([zb, fb, zb, fb])[None] + nrm((L, 4, H), 0.1)
    ml_norm_g = 1.0 + nrm((L, D_MLSTM), 0.02)
    w_ml_out = nrm((L, D_MLSTM, D_MODEL), D_MLSTM ** -0.5)
    da_qnorm_g = 1.0 + nrm((L, DIFF_QK_HEAD), 0.02)
    da_knorm_g = 1.0 + nrm((L, DIFF_QK_HEAD), 0.02)
    da_lambda = nrm((L, 4, DIFF_QK_HEAD), 0.1)
    da_subln_g = 1.0 + nrm((L, DIFF_V_HEAD), 0.02)
    w_da_out = nrm((L, D_DIFF, D_MODEL), D_DIFF ** -0.5)
    w_out = nrm((L, D_MODEL, D_MODEL), D_MODEL ** -0.5)
    return {'x': x, 'c': c, 'ctx': ctx, 'c_ctx': c_ctx, 'norm_g': norm_g, 'ada_w': ada_w,
            'ada_b': ada_b, 'w_in': w_in, 'ssm_lam_re': ssm_lam_re, 'ssm_lam_im': ssm_lam_im,
            'ssm_log_step': ssm_log_step, 'ssm_b_re': ssm_b_re, 'ssm_b_im': ssm_b_im,
            'ssm_c_re': ssm_c_re, 'ssm_c_im': ssm_c_im, 'ssm_d': ssm_d, 'ssm_glu_w': ssm_glu_w,
            'ssm_glu_b': ssm_glu_b, 'w_ssm_out': w_ssm_out, 'ml_conv_w': ml_conv_w,
            'ml_conv_b': ml_conv_b, 'ml_gate_b': ml_gate_b, 'ml_norm_g': ml_norm_g,
            'w_ml_out': w_ml_out, 'da_qnorm_g': da_qnorm_g, 'da_knorm_g': da_knorm_g,
            'da_lambda': da_lambda, 'da_subln_g': da_subln_g, 'w_da_out': w_da_out, 'w_out': w_out}


def reference(x, c, ctx, c_ctx, norm_g, ada_w, ada_b, w_in, ssm_lam_re, ssm_lam_im, ssm_log_step,
              ssm_b_re, ssm_b_im, ssm_c_re, ssm_c_im, ssm_d, ssm_glu_w, ssm_glu_b, w_ssm_out,
              ml_conv_w, ml_conv_b, ml_gate_b, ml_norm_g, w_ml_out, da_qnorm_g, da_knorm_g,
              da_lambda, da_subln_g, w_da_out, w_out):
    rope = axial_rope_tables(x.shape[1])
    c_act = jax.nn.silu(c)
    cc_act = jax.nn.silu(c_ctx)
    xl, xc = x, ctx
    for li in range(DEPTH):
        with_ctx = li < DEPTH - 1
        sh_l, sc_l, gt_l = jnp.split(c_act @ ada_w[li] + ada_b[li], 3, axis=-1)
        sh_c, sc_c, gt_c = jnp.split(cc_act @ ada_w[li] + ada_b[li], 3, axis=-1)
        h_l = rmsnorm(xl, norm_g[li]) * (1.0 + sc_l[:, None]) + sh_l[:, None]
        h_c = rmsnorm(xc, norm_g[li]) * (1.0 + sc_c) + sh_c
        (su_l, sz_l, mqk_l, mv_l, mo_l, mz_l, mg_l, dq_l, dk_l, dv_l, dz_l, gl_l) = split_cols(h_l @ w_in[li])
        (su_c, sz_c, mqk_c, mv_c, mo_c, mz_c, mg_c, dq_c, dk_c, dv_c, dz_c, gl_c) = split_cols(h_c @ w_in[li])

        ya_c, ya_l = s5_branch(su_c, su_l, ssm_lam_re[li], ssm_lam_im[li], ssm_log_step[li],
                               ssm_b_re[li], ssm_b_im[li], ssm_c_re[li], ssm_c_im[li], ssm_d[li],
                               ssm_glu_w[li], ssm_glu_b[li], with_ctx)
        yb_c, yb_l = mlstm_branch((mqk_c, mv_c, mo_c, mg_c), (mqk_l, mv_l, mo_l, mg_l),
                                  ml_conv_w[li], ml_conv_b[li], ml_gate_b[li], ml_norm_g[li], with_ctx)
        yc_c, yc_l = diff_attn_branch((dq_c, dk_c, dv_c), (dq_l, dk_l, dv_l), da_qnorm_g[li],
                                      da_knorm_g[li], da_lambda[li], da_subln_g[li], rope, li, with_ctx)

        out_l = merge_branches((ya_l, yb_l, yc_l), (sz_l, mz_l, dz_l), gl_l,
                               w_ssm_out[li], w_ml_out[li], w_da_out[li], w_out[li])
        xl = xl + gt_l[:, None] * out_l
        if with_ctx:
            out_c = merge_branches((ya_c, yb_c, yc_c), (sz_c, mz_c, dz_c), gl_c,
                                   w_ssm_out[li], w_ml_out[li], w_da_out[li], w_out[li])
            xc = xc + gt_c * out_c
    return xl
```

```python
import functools
import math

import jax
import jax.numpy as jnp
from jax import lax
from jax.experimental import pallas as pl
from jax.experimental.pallas import tpu as pltpu

F32 = jnp.float32
BF16 = jnp.bfloat16
HIGHEST = lax.Precision.HIGHEST

EPS = 1e-6
GRID_W = 64
ROPE_BASE = 10000.0

D_SSM = 512
SSM_GROUP = 16
N_SSM_GROUPS = D_SSM // SSM_GROUP
SSM_STATE = 64
S5_CHUNK = 16
S5_K = S5_CHUNK * SSM_GROUP
D_MLSTM = 512
N_MLSTM_HEADS = 4
MLSTM_HEAD = D_MLSTM // N_MLSTM_HEADS
MLSTM_CHUNK = 128
QK_CONV = 3
N_DIFF_HEADS = 4
DIFF_QK_HEAD = 64
DIFF_V_HEAD = 2 * DIFF_QK_HEAD
D_DIFF = N_DIFF_HEADS * DIFF_V_HEAD
N_BRANCH = 3

ROW_TILE = 256
GATE_PAD = 128
MOD_ROWS_ALIGN = 8

COL_GL = 0
VMEM_LIMIT = 56 * 1024 * 1024


def _cols(d_model):
    off = {}
    o = N_BRANCH * d_model
    for name, w in (("su", D_SSM), ("sz", D_SSM), ("mqk", 2 * D_MLSTM), ("mv", D_MLSTM), ("mo", D_MLSTM),
                    ("mz", D_MLSTM), ("dq", D_DIFF), ("dk", D_DIFF), ("dv", D_DIFF), ("dz", D_DIFF)):
        off[name] = o
        o += w
    off["total"] = o
    return off


def _cparams(sem):
    return pltpu.CompilerParams(dimension_semantics=sem, vmem_limit_bytes=VMEM_LIMIT)


def _mods_kernel(c_ref, w_ref, b_ref, o_ref):
    c = c_ref[...]
    act = c * jax.nn.sigmoid(c)
    o_ref[0] = jnp.dot(act, w_ref[0], preferred_element_type=F32, precision=HIGHEST) + b_ref[0]


def _modulation(cvec, ada_w, ada_b):
    depth, d, d3 = ada_w.shape
    rows = cvec.shape[0]
    return pl.pallas_call(
        _mods_kernel,
        grid=(depth, d3 // d),
        in_specs=[pl.BlockSpec((rows, d), lambda l, j: (0, 0)),
                  pl.BlockSpec((1, d, d), lambda l, j: (l, 0, j)),
                  pl.BlockSpec((1, 1, d), lambda l, j: (l, 0, j))],
        out_specs=pl.BlockSpec((1, rows, d), lambda l, j: (l, 0, j)),
        out_shape=jax.ShapeDtypeStruct((depth, rows, d3), F32),
        compiler_params=_cparams(("arbitrary", "arbitrary")),
        name="adaln_modulation",
    )(cvec, ada_w, ada_b.reshape(depth, 1, d3))


def _inproj_kernel(x_ref, mod_ref, g_ref, w_ref, wg_ref, o_ref, og_ref, *, d_model, col_chunk):
    x = x_ref[0]
    ms = jnp.mean(x * x, axis=-1, keepdims=True)
    y = x * lax.rsqrt(ms + EPS) * g_ref[...]
    sh = mod_ref[0, :, 0:d_model]
    sc = mod_ref[0, :, d_model:2 * d_model]
    hb = (y * (1.0 + sc) + sh).astype(BF16)
    n_out = o_ref.shape[-1]
    for c0 in range(0, n_out, col_chunk):
        o_ref[0, :, c0:c0 + col_chunk] = jnp.dot(
            hb, w_ref[:, c0:c0 + col_chunk], preferred_element_type=F32).astype(BF16)
    og_ref[0] = jnp.dot(hb, wg_ref[...], preferred_element_type=F32)


def _inproj(x_all, mods3, norm_g, w_p, w_g, n_batch, lc):
    b, lt, d = x_all.shape
    n_out = w_p.shape[1]
    ctx_tiles = lc // ROW_TILE
    kern = functools.partial(_inproj_kernel, d_model=d, col_chunk=512)
    return pl.pallas_call(
        kern,
        grid=(b, lt // ROW_TILE),
        in_specs=[pl.BlockSpec((1, ROW_TILE, d), lambda bi, i: (bi, i, 0)),
                  pl.BlockSpec((1, 1, 3 * d), lambda bi, i: (jnp.where(i < ctx_tiles, n_batch, bi), 0, 0)),
                  pl.BlockSpec((1, d), lambda bi, i: (0, 0)),
                  pl.BlockSpec((d, n_out), lambda bi, i: (0, 0), pipeline_mode=pl.Buffered(1)),
                  pl.BlockSpec((d, GATE_PAD), lambda bi, i: (0, 0))],
        out_specs=[pl.BlockSpec((1, ROW_TILE, n_out), lambda bi, i: (bi, i, 0)),
                   pl.BlockSpec((1, ROW_TILE, GATE_PAD), lambda bi, i: (bi, i, 0))],
        out_shape=[jax.ShapeDtypeStruct((b, lt, n_out), BF16),
                   jax.ShapeDtypeStruct((b, lt, GATE_PAD), F32)],
        compiler_params=_cparams(("parallel", "parallel")),
        name="adaln_inproj",
    )(x_all, mods3, norm_g.reshape(1, d), w_p, w_g)


def _s5_operators(lam_re, lam_im, log_step, b_re, b_im, c_re, c_im, d_skip):
    t = S5_CHUNK
    g, p, n = N_SSM_GROUPS, SSM_STATE, SSM_GROUP
    hp = HIGHEST
    tin = jnp.zeros((g, t, n, t, n), F32)
    m1_states, m2_rows, decs = [], [], []
    s_idx = jnp.arange(t)
    for d in range(2):
        lre = jnp.minimum(lam_re[d].astype(F32), -1e-4)
        lim = lam_im[d].astype(F32)
        dt = jnp.exp(log_step[d].astype(F32))[:, None]
        ld_re, ld_im = lre * dt, lim * dt
        mag = jnp.exp(ld_re)
        ab_re, ab_im = mag * jnp.cos(ld_im), mag * jnp.sin(ld_im)
        den = lre * lre + lim * lim
        num_re, num_im = ab_re - 1.0, ab_im
        coef_re = (num_re * lre + num_im * lim) / den
        coef_im = (num_im * lre - num_re * lim) / den
        bre, bim = b_re[d].astype(F32), b_im[d].astype(F32)
        bb_re = coef_re[..., None] * bre - coef_im[..., None] * bim
        bb_im = coef_re[..., None] * bim + coef_im[..., None] * bre
        cre, cim = c_re[d].astype(F32), c_im[d].astype(F32)
        k_pow = jnp.arange(t + 1, dtype=F32)[:, None, None]
        pmag = jnp.exp(k_pow * ld_re)
        pw_re, pw_im = pmag * jnp.cos(k_pow * ld_im), pmag * jnp.sin(k_pow * ld_im)
        ab_b_re = pw_re[..., None] * bb_re - pw_im[..., None] * bb_im
        ab_b_im = pw_re[..., None] * bb_im + pw_im[..., None] * bb_re
        kern = (jnp.einsum("gop,kgpn->kgon", cre, ab_b_re, precision=hp)
                - jnp.einsum("gop,kgpn->kgon", cim, ab_b_im, precision=hp))
        lag = (s_idx[None, :] - s_idx[:, None]) if d == 0 else (s_idx[:, None] - s_idx[None, :])
        valid = lag >= 0
        kk = kern[jnp.clip(lag, 0, t)]
        kk = jnp.where(valid[:, :, None, None, None], kk, 0.0)
        tin = tin + jnp.transpose(kk, (2, 0, 4, 1, 3))
        pw_idx = (t - 1 - s_idx) if d == 0 else s_idx
        st_re = jnp.transpose(ab_b_re[pw_idx], (1, 0, 3, 2))
        st_im = jnp.transpose(ab_b_im[pw_idx], (1, 0, 3, 2))
        m1_states.append(jnp.concatenate([st_re, st_im], -1).reshape(g, t * n, 2 * p))
        m1_states.append(jnp.concatenate([st_im, st_re], -1).reshape(g, t * n, 2 * p))
        rp = (s_idx + 1) if d == 0 else (t - s_idx)
        ca_re = cre[None] * pw_re[rp][:, :, None, :] - cim[None] * pw_im[rp][:, :, None, :]
        ca_im = cre[None] * pw_im[rp][:, :, None, :] + cim[None] * pw_re[rp][:, :, None, :]
        rd = jnp.concatenate([jnp.transpose(ca_re, (1, 3, 0, 2)), -jnp.transpose(ca_im, (1, 3, 0, 2))], 1)
        m2_rows.append(rd.reshape(g, 2 * p, t * n))
        decs.append(jnp.concatenate([pw_re[t], pw_re[t]], -1))
        decs.append(jnp.concatenate([-pw_im[t], pw_im[t]], -1))
    eye = jnp.eye(t * n, dtype=F32).reshape(1, t, n, t, n)
    tin = tin + eye * d_skip.astype(F32).reshape(g, 1, n, 1, 1)
    m1 = jnp.concatenate([tin.reshape(g, t * n, t * n)] + m1_states, axis=-1)
    m2 = jnp.concatenate(m2_rows, axis=1)
    dec = jnp.stack(decs, axis=1)
    return m1.astype(BF16), m2.astype(BF16), dec


def _s5_kernel(x_ref, m1_ref, m2_ref, dec_ref, o_ref, p_ref, h_ref, *, n_batch, n_ctx_chunks, n_chunks):
    k, sw = S5_K, 2 * SSM_STATE
    p_ref[...] = jnp.dot(x_ref[0], m1_ref[0], preferred_element_type=F32)
    a1f, a2f = dec_ref[0, 0:1, :], dec_ref[0, 1:2, :]
    a1b, a2b = dec_ref[0, 2:3, :], dec_ref[0, 3:4, :]
    zero = jnp.zeros((n_batch, sw), F32)

    def step(j, carry):
        hf, hfs, hb, hbs = carry
        cb = jnp.where(j < n_ctx_chunks, n_ctx_chunks - 1 - j, n_chunks - 1 + n_ctx_chunks - j)
        rf = pl.multiple_of(j * n_batch, n_batch)
        rb = pl.multiple_of(cb * n_batch, n_batch)
        h_ref[pl.ds(rf, n_batch), 0:sw] = hf
        h_ref[pl.ds(rb, n_batch), sw:2 * sw] = hb
        sf = p_ref[pl.ds(rf, n_batch), k:k + sw]
        sfs = p_ref[pl.ds(rf, n_batch), k + sw:k + 2 * sw]
        sb = p_ref[pl.ds(rb, n_batch), k + 2 * sw:k + 3 * sw]
        sbs = p_ref[pl.ds(rb, n_batch), k + 3 * sw:k + 4 * sw]
        return (a1f * hf + a2f * hfs + sf, a1f * hfs - a2f * hf + sfs,
                a1b * hb + a2b * hbs + sb, a1b * hbs - a2b * hb + sbs)

    lax.fori_loop(0, n_chunks, step, (zero, zero, zero, zero))
    inter = jnp.dot(h_ref[...].astype(BF16), m2_ref[0], preferred_element_type=F32)
    o_ref[0] = (p_ref[:, 0:k] + inter).astype(BF16)


def _s5_mixer(x_blocks, m1, m2, dec, n_batch, n_ctx_chunks):
    g, r, k = x_blocks.shape
    n_chunks = r // n_batch
    kern = functools.partial(_s5_kernel, n_batch=n_batch, n_ctx_chunks=n_ctx_chunks, n_chunks=n_chunks)
    return pl.pallas_call(
        kern,
        grid=(g,),
        in_specs=[pl.BlockSpec((1, r, k), lambda i: (i, 0, 0)),
                  pl.BlockSpec((1, k, m1.shape[2]), lambda i: (i, 0, 0)),
                  pl.BlockSpec((1, m2.shape[1], k), lambda i: (i, 0, 0)),
                  pl.BlockSpec((1, 4, 2 * SSM_STATE), lambda i: (i, 0, 0))],
        out_specs=pl.BlockSpec((1, r, k), lambda i: (i, 0, 0)),
        out_shape=jax.ShapeDtypeStruct((g, r, k), BF16),
        scratch_shapes=[pltpu.VMEM((r, m1.shape[2]), F32), pltpu.VMEM((r, 4 * SSM_STATE), F32)],
        compiler_params=_cparams(("parallel",)),
        name="s5_mixer",
    )(x_blocks, m1, m2, dec)


def _qkconv_kernel(x_ref, w_ref, b_ref, s_ref, o_ref, *, lc):
    x = x_ref[0].astype(F32)
    lt = x.shape[0]
    row = lax.broadcasted_iota(jnp.int32, (lt, 1), 0)
    prev = jnp.where((row == 0) | (row == lc), 0.0, pltpu.roll(x, 1, axis=0))
    nxt = jnp.where((row == lc - 1) | (row == lt - 1), 0.0, pltpu.roll(x, lt - 1, axis=0))
    y = w_ref[0:1, :] * prev + w_ref[1:2, :] * x + w_ref[2:3, :] * nxt + b_ref[...]
    o_ref[0] = (y * jax.nn.sigmoid(y) * s_ref[...]).astype(BF16)


def _qkconv(proj, conv_w, conv_b, post_scale, col0, lc):
    b, lt, _ = proj.shape
    width = conv_w.shape[1]
    cw = 256
    blk0 = col0 // cw
    kern = functools.partial(_qkconv_kernel, lc=lc)
    return pl.pallas_call(
        kern,
        grid=(b, width // cw),
        in_specs=[pl.BlockSpec((1, lt, cw), lambda bi, j: (bi, 0, blk0 + j)),
                  pl.BlockSpec((QK_CONV, cw), lambda bi, j: (0, j)),
                  pl.BlockSpec((1, cw), lambda bi, j: (0, j)),
                  pl.BlockSpec((1, cw), lambda bi, j: (0, j))],
        out_specs=pl.BlockSpec((1, lt, cw), lambda bi, j: (bi, 0, j)),
        out_shape=jax.ShapeDtypeStruct((b, lt, width), BF16),
        compiler_params=_cparams(("parallel", "parallel")),
        name="mlstm_qk_conv",
    )(proj, conv_w, conv_b.reshape(1, width), post_scale)


def _log_sigmoid(x):
    return jnp.minimum(x, 0.0) - jnp.log1p(jnp.exp(-jnp.abs(x)))


def _mlstm_chunk(q, k, v, icol, fcol, irow, frow, state, rev):
    c0, n0, m0 = state
    t = MLSTM_CHUNK
    r_i = lax.broadcasted_iota(jnp.int32, (t, t), 0)
    c_i = lax.broadcasted_iota(jnp.int32, (t, t), 1)
    tri = (c_i >= r_i) if rev else (c_i <= r_i)
    lf_col, lf_row = _log_sigmoid(fcol), _log_sigmoid(frow)
    f_col = jnp.sum(jnp.where(tri, lf_row, 0.0), axis=1, keepdims=True)
    tri_t = (r_i >= c_i) if rev else (r_i <= c_i)
    f_row = jnp.sum(jnp.where(tri_t, lf_col, 0.0), axis=0, keepdims=True)
    f_end = f_row[:, 0:1] if rev else f_row[:, t - 1:t]
    log_d = jnp.where(tri, f_col - f_row + irow, -jnp.inf)
    log_inter = f_col + m0
    m_t = jnp.maximum(jnp.max(log_d, axis=1, keepdims=True), log_inter)
    qk = lax.dot_general(q, k, (((1,), (1,)), ((), ())), preferred_element_type=F32)
    s_qk = qk * jnp.exp(log_d - m_t)
    e_inter = jnp.exp(log_inter - m_t)
    num = (jnp.dot(s_qk.astype(BF16), v, preferred_element_type=F32)
           + e_inter * jnp.dot(q, c0.astype(BF16), preferred_element_type=F32))
    den = (jnp.sum(s_qk, axis=1, keepdims=True)
           + e_inter * jnp.sum(q.astype(F32) * n0, axis=1, keepdims=True))
    h = num / jnp.maximum(jnp.abs(den), jnp.exp(-m_t))
    w_end = f_end - f_col + icol
    m_loc = jnp.max(w_end, axis=0, keepdims=True)
    e_end = jnp.exp(w_end - m_loc)
    ev = (e_end * v.astype(F32)).astype(BF16)
    d_c = lax.dot_general(k, ev, (((0,), (0,)), ((), ())), preferred_element_type=F32)
    d_n = jnp.sum(e_end * k.astype(F32), axis=0, keepdims=True)
    m_new = jnp.maximum(f_end + m0, m_loc)
    a = jnp.exp(f_end + m0 - m_new)
    bb = jnp.exp(m_loc - m_new)
    return h, (a * c0 + bb * d_c, a * n0 + bb * d_n, m_new)


def _mlstm_kernel(q_ref, k_ref, v_ref, o_ref, gc_ref, gr_ref, bc_ref, br_ref, g_ref, out_ref, acc_ref,
                  *, n_ctx_chunks, n_chunks):
    t, dh = MLSTM_CHUNK, MLSTM_HEAD
    acc_ref[...] = jnp.zeros_like(acc_ref)
    state0 = (jnp.zeros((dh, dh), F32), jnp.zeros((1, dh), F32), jnp.zeros((1, 1), F32))

    def run(j, state, rev):
        if rev:
            c = jnp.where(j < n_ctx_chunks, n_ctx_chunks - 1 - j, n_chunks - 1 + n_ctx_chunks - j)
        else:
            c = j
        r0 = pl.multiple_of(c * t, t)
        gi = 2 if rev else 0
        icol = gc_ref[0, 0, pl.ds(r0, t), gi:gi + 1] + bc_ref[0, :, gi:gi + 1]
        fcol = gc_ref[0, 0, pl.ds(r0, t), gi + 1:gi + 2] + bc_ref[0, :, gi + 1:gi + 2]
        irow = gr_ref[0, 0, gi:gi + 1, pl.ds(r0, t)] + br_ref[0, gi:gi + 1, :]
        frow = gr_ref[0, 0, gi + 1:gi + 2, pl.ds(r0, t)] + br_ref[0, gi + 1:gi + 2, :]
        h, state = _mlstm_chunk(q_ref[0, pl.ds(r0, t), :], k_ref[0, pl.ds(r0, t), :], v_ref[0, pl.ds(r0, t), :],
                                icol, fcol, irow, frow, state, rev)
        acc_ref[pl.ds(r0, t), :] += h
        return state

    def step(j, carry):
        sf, sb = carry
        return run(j, sf, False), run(j, sb, True)

    lax.fori_loop(0, n_chunks, step, (state0, state0))
    hsum = acc_ref[...]
    y = hsum * lax.rsqrt(jnp.mean(hsum * hsum, axis=-1, keepdims=True) + EPS) * g_ref[...]
    out_ref[0] = (y * jax.nn.sigmoid(o_ref[0].astype(F32))).astype(BF16)


def _mlstm(qk, proj, gates_col, gates_row, bias_col, bias_row, norm_g, col_v, col_o, lc):
    b, lt, _ = qk.shape
    h, dh, t = N_MLSTM_HEADS, MLSTM_HEAD, MLSTM_CHUNK
    vb, ob = col_v // dh, col_o // dh
    kern = functools.partial(_mlstm_kernel, n_ctx_chunks=lc // t, n_chunks=lt // t)
    return pl.pallas_call(
        kern,
        grid=(b, h),
        in_specs=[pl.BlockSpec((1, lt, dh), lambda bi, hi: (bi, 0, hi)),
                  pl.BlockSpec((1, lt, dh), lambda bi, hi: (bi, 0, h + hi)),
                  pl.BlockSpec((1, lt, dh), lambda bi, hi: (bi, 0, vb + hi)),
                  pl.BlockSpec((1, lt, dh), lambda bi, hi: (bi, 0, ob + hi)),
                  pl.BlockSpec((1, 1, lt, 4), lambda bi, hi: (bi, hi, 0, 0)),
                  pl.BlockSpec((1, 1, 4, lt), lambda bi, hi: (bi, hi, 0, 0)),
                  pl.BlockSpec((1, 1, 4), lambda bi, hi: (hi, 0, 0)),
                  pl.BlockSpec((1, 4, 1), lambda bi, hi: (hi, 0, 0)),
                  pl.BlockSpec((1, dh), lambda bi, hi: (0, hi))],
        out_specs=pl.BlockSpec((1, lt, dh), lambda bi, hi: (bi, 0, hi)),
        out_shape=jax.ShapeDtypeStruct((b, lt, h * dh), BF16),
        scratch_shapes=[pltpu.VMEM((lt, dh), F32)],
        compiler_params=_cparams(("parallel", "parallel")),
        name="mlstm_mixer",
    )(qk, qk, proj, proj, gates_col, gates_row, bias_col, bias_row, norm_g.reshape(1, h * dh))


def _segment_mean_sq(x, ones_bd, seg):
    ss = x * x
    hi = ss.astype(BF16)
    lo = (ss - hi.astype(F32)).astype(BF16)
    tot = jnp.dot(hi, ones_bd, preferred_element_type=F32) + jnp.dot(lo, ones_bd, preferred_element_type=F32)
    return tot * (1.0 / seg)


def _qkprep_kernel(q_ref, k_ref, cos_ref, sin_ref, qg_ref, kg_ref, ones_ref, qo_ref, ko_ref):
    cos, sin = cos_ref[...], sin_ref[...]
    lane = lax.broadcasted_iota(jnp.int32, (1, 128), 1)
    first_half = (lane // 16) % 2 == 0

    def prep(ref, g_ref, scale):
        x = ref[0].astype(F32)
        y = x * lax.rsqrt(_segment_mean_sq(x, ones_ref[...], DIFF_QK_HEAD) + EPS) * g_ref[...]
        outs = []
        for c0 in range(0, y.shape[1], 128):
            yb = y[:, c0:c0 + 128]
            partner = jnp.where(first_half, pltpu.roll(yb, 128 - 16, axis=1), pltpu.roll(yb, 16, axis=1))
            outs.append(yb * cos + partner * sin)
        return (jnp.concatenate(outs, axis=1) * scale).astype(BF16)

    qo_ref[0] = prep(q_ref, qg_ref, DIFF_QK_HEAD ** -0.5)
    ko_ref[0] = prep(k_ref, kg_ref, 1.0)


def _qkprep(proj, cos_t, sin_t, qg, kg, ones_bd, col_q, col_k):
    b, lt, _ = proj.shape
    w = D_DIFF
    qb, kb = col_q // w, col_k // w
    return pl.pallas_call(
        _qkprep_kernel,
        grid=(b, lt // ROW_TILE),
        in_specs=[pl.BlockSpec((1, ROW_TILE, w), lambda bi, i: (bi, i, qb)),
                  pl.BlockSpec((1, ROW_TILE, w), lambda bi, i: (bi, i, kb)),
                  pl.BlockSpec((ROW_TILE, 128), lambda bi, i: (i, 0)),
                  pl.BlockSpec((ROW_TILE, 128), lambda bi, i: (i, 0)),
                  pl.BlockSpec((1, w), lambda bi, i: (0, 0)),
                  pl.BlockSpec((1, w), lambda bi, i: (0, 0)),
                  pl.BlockSpec((w, w), lambda bi, i: (0, 0))],
        out_specs=[pl.BlockSpec((1, ROW_TILE, w), lambda bi, i: (bi, i, 0)),
                   pl.BlockSpec((1, ROW_TILE, w), lambda bi, i: (bi, i, 0))],
        out_shape=[jax.ShapeDtypeStruct((b, lt, w), BF16), jax.ShapeDtypeStruct((b, lt, w), BF16)],
        compiler_params=_cparams(("parallel", "parallel")),
        name="diffattn_qk_prep",
    )(proj, proj, cos_t, sin_t, qg, kg, ones_bd)


def _attn_kernel(sc_ref, q_ref, k_ref, v_ref, lam_ref, g_ref, o_ref, *, lc, ctx_tiles):
    i = pl.program_id(2)
    lp = lam_ref[...]
    lam = (jnp.exp(jnp.sum(lp[0:1] * lp[1:2], axis=1, keepdims=True))
           - jnp.exp(jnp.sum(lp[2:3] * lp[3:4], axis=1, keepdims=True)) + sc_ref[0])
    lane = lax.broadcasted_iota(jnp.int32, (1, DIFF_V_HEAD), 1)

    def attend(nk):
        q = q_ref[0]
        k = k_ref[0, 0:nk, :]
        v = v_ref[0, 0:nk, :]
        outs = []
        for c in range(2):
            keep = (lane < DIFF_QK_HEAD) if c == 0 else (lane >= DIFF_QK_HEAD)
            qc = jnp.where(keep, q, jnp.zeros_like(q))
            s = lax.dot_general(qc, k, (((1,), (1,)), ((), ())), preferred_element_type=F32)
            p = jnp.exp(s - jnp.max(s, axis=1, keepdims=True))
            d = jnp.sum(p, axis=1, keepdims=True)
            outs.append(jnp.dot(p.astype(BF16), v, preferred_element_type=F32) / d)
        o = outs[0] - lam * outs[1]
        y = o * lax.rsqrt(jnp.mean(o * o, axis=-1, keepdims=True) + EPS) * g_ref[...]
        o_ref[0] = (y * sc_ref[1]).astype(BF16)

    @pl.when(i < ctx_tiles)
    def _():
        attend(lc)

    @pl.when(i >= ctx_tiles)
    def _():
        attend(k_ref.shape[1])


def _attention(scal, qn, kn, proj, lam_p, subln_g, col_v, lc):
    b, lt, _ = qn.shape
    h, dv = N_DIFF_HEADS, DIFF_V_HEAD
    vb = col_v // dv
    kern = functools.partial(_attn_kernel, lc=lc, ctx_tiles=lc // ROW_TILE)
    return pl.pallas_call(
        kern,
        grid=(b, h, lt // ROW_TILE),
        in_specs=[pl.BlockSpec(memory_space=pltpu.SMEM),
                  pl.BlockSpec((1, ROW_TILE, dv), lambda bi, hi, i: (bi, i, hi)),
                  pl.BlockSpec((1, lt, dv), lambda bi, hi, i: (bi, 0, hi)),
                  pl.BlockSpec((1, lt, dv), lambda bi, hi, i: (bi, 0, vb + hi)),
                  pl.BlockSpec((4, DIFF_QK_HEAD), lambda bi, hi, i: (0, 0)),
                  pl.BlockSpec((1, dv), lambda bi, hi, i: (0, 0))],
        out_specs=pl.BlockSpec((1, ROW_TILE, dv), lambda bi, hi, i: (bi, i, hi)),
        out_shape=jax.ShapeDtypeStruct((b, lt, h * dv), BF16),
        compiler_params=_cparams(("parallel", "parallel", "parallel")),
        name="diff_attention",
    )(scal, qn, kn, proj, lam_p, subln_g.reshape(1, dv))


def _silu(x):
    return x * jax.nn.sigmoid(x)


def _merge_kernel(x_ref, mod_ref, y5_ref, sz_ref, hb_ref, mz_ref, oc_ref, dz_ref, gl_ref,
                  gw_ref, gb_ref, wa_ref, wb_ref, wc_ref, wo_ref, o_ref, *, d_model):
    f = lambda r: r[0].astype(F32)
    y = f(y5_ref)
    ge = 0.5 * y * (1.0 + jnp.tanh(math.sqrt(2.0 / math.pi) * (y + 0.044715 * (y * y * y))))
    t = jnp.dot(ge.astype(BF16), gw_ref[...], preferred_element_type=F32) + gb_ref[...]
    ya = t[:, 0:D_SSM] * jax.nn.sigmoid(t[:, D_SSM:2 * D_SSM])
    pa = jnp.dot((ya * _silu(f(sz_ref))).astype(BF16), wa_ref[...], preferred_element_type=F32)
    pb = jnp.dot((f(hb_ref) * _silu(f(mz_ref))).astype(BF16), wb_ref[...], preferred_element_type=F32)
    pc = jnp.dot((f(oc_ref) * _silu(f(dz_ref))).astype(BF16), wc_ref[...], preferred_element_type=F32)
    gates = jax.nn.sigmoid(f(gl_ref))
    m = (gates[:, 0:d_model] * pa + gates[:, d_model:2 * d_model] * pb
         + gates[:, 2 * d_model:3 * d_model] * pc)
    out = jnp.dot(m.astype(BF16), wo_ref[...], preferred_element_type=F32)
    o_ref[0] = x_ref[0] + mod_ref[0, :, 2 * d_model:3 * d_model] * out


def _merge(x_all, mods3, y5, hb, oc, proj, glu_w, glu_b, wa, wb, wc, wo, cols, n_batch, lc, skip_ctx):
    b, lt, d = x_all.shape
    ctx_tiles = lc // ROW_TILE
    t0 = ctx_tiles if skip_ctx else 0
    w = D_SSM
    blk = lambda name: cols[name] // w
    tok = lambda j: pl.BlockSpec((1, ROW_TILE, w), lambda bi, i: (bi, i + t0, j))
    full = lambda a: pl.BlockSpec(a.shape, lambda bi, i: (0,) * a.ndim)
    kern = functools.partial(_merge_kernel, d_model=d)
    return pl.pallas_call(
        kern,
        grid=(b, lt // ROW_TILE - t0),
        in_specs=[pl.BlockSpec((1, ROW_TILE, d), lambda bi, i: (bi, i + t0, 0)),
                  pl.BlockSpec((1, 1, 3 * d), lambda bi, i: (jnp.where(i + t0 < ctx_tiles, n_batch, bi), 0, 0)),
                  tok(0), tok(blk("sz")), tok(0), tok(blk("mz")), tok(0), tok(blk("dz")),
                  pl.BlockSpec((1, ROW_TILE, N_BRANCH * d), lambda bi, i: (bi, i + t0, 0)),
                  full(glu_w), full(glu_b), full(wa), full(wb), full(wc), full(wo)],
        out_specs=pl.BlockSpec((1, ROW_TILE, d), lambda bi, i: (bi, i, 0)),
        out_shape=jax.ShapeDtypeStruct((b, lt - t0 * ROW_TILE, d), F32),
        compiler_params=_cparams(("parallel", "parallel")),
        name="merge_outproj",
    )(x_all, mods3, y5, proj, hb, proj, oc, proj, proj, glu_w, glu_b, wa, wb, wc, wo)


def _rope_tables(lc, l):
    half = DIFF_QK_HEAD // 2
    rows = l // GRID_W
    row = jnp.repeat(jnp.arange(rows), GRID_W).astype(F32)
    col = jnp.tile(jnp.arange(GRID_W), rows).astype(F32)
    inv = jnp.power(ROPE_BASE, -jnp.arange(0, half, 2, dtype=F32) / half)
    ang_r, ang_c = row[:, None] * inv, col[:, None] * inv
    cos64 = jnp.concatenate([jnp.cos(ang_r), jnp.cos(ang_r), jnp.cos(ang_c), jnp.cos(ang_c)], axis=1)
    sin64 = jnp.concatenate([-jnp.sin(ang_r), jnp.sin(ang_r), -jnp.sin(ang_c), jnp.sin(ang_c)], axis=1)
    cos_t = jnp.concatenate([jnp.ones((lc, DIFF_QK_HEAD), F32), cos64], axis=0)
    sin_t = jnp.concatenate([jnp.zeros((lc, DIFF_QK_HEAD), F32), sin64], axis=0)
    return jnp.tile(cos_t, (1, 2)), jnp.tile(sin_t, (1, 2))


def kernel(x, c, ctx, c_ctx, norm_g, ada_w, ada_b, w_in, ssm_lam_re, ssm_lam_im, ssm_log_step, ssm_b_re, ssm_b_im, ssm_c_re, ssm_c_im, ssm_d, ssm_glu_w, ssm_glu_b, w_ssm_out, ml_conv_w, ml_conv_b, ml_gate_b, ml_norm_g, w_ml_out, da_qnorm_g, da_knorm_g, da_lambda, da_subln_g, w_da_out, w_out):
    n_batch, l, d = x.shape
    lc = ctx.shape[1]
    lt = lc + l
    depth = w_in.shape[0]
    cols = _cols(d)
    assert lc % ROW_TILE == 0 and l % ROW_TILE == 0 and lc % MLSTM_CHUNK == 0 and n_batch % 8 == 0
    g, n, t5, h = N_SSM_GROUPS, SSM_GROUP, S5_CHUNK, N_MLSTM_HEADS

    x_all = jnp.concatenate([ctx, x], axis=1)

    mod_rows = -(-(n_batch + 1) // MOD_ROWS_ALIGN) * MOD_ROWS_ALIGN
    cvec = jnp.zeros((mod_rows, d), F32).at[:n_batch].set(c).at[n_batch].set(c_ctx)
    mods = _modulation(cvec, ada_w, ada_b)

    g0 = 2 * D_SSM + 5 * D_MLSTM
    g1 = g0 + 4 * h
    m0 = g1 + 4 * D_DIFF
    w_p = jnp.concatenate([w_in[:, :, m0:], w_in[:, :, :g0], w_in[:, :, g1:m0]], axis=2).astype(BF16)
    w_g = jnp.pad(w_in[:, :, g0:g1], ((0, 0), (0, 0), (0, GATE_PAD - 4 * h))).astype(BF16)

    cos_t, sin_t = _rope_tables(lc, l)
    ones_bd = jnp.kron(jnp.eye(D_DIFF // DIFF_QK_HEAD, dtype=F32),
                       jnp.ones((DIFF_QK_HEAD, DIFF_QK_HEAD), F32)).astype(BF16)
    qk_scale = jnp.concatenate([jnp.ones((1, D_MLSTM), F32), jnp.full((1, D_MLSTM), MLSTM_HEAD ** -0.5, F32)], 1)

    for li in range(depth):
        last = li == depth - 1
        mods3 = mods[li].reshape(mod_rows, 1, 3 * d)
        proj, gates = _inproj(x_all, mods3, norm_g[li], w_p[li], w_g[li], n_batch, lc)

        su = proj[:, :, cols["su"]:cols["su"] + D_SSM]
        xb = su.reshape(n_batch, lt // t5, t5, g, n).transpose(3, 1, 0, 2, 4).reshape(g, (lt // t5) * n_batch, t5 * n)
        m1, m2, dec = _s5_operators(ssm_lam_re[li], ssm_lam_im[li], ssm_log_step[li], ssm_b_re[li], ssm_b_im[li],
                                    ssm_c_re[li], ssm_c_im[li], ssm_d[li])
        yb = _s5_mixer(xb, m1, m2, dec, n_batch, lc // t5)
        y5 = yb.reshape(g, lt // t5, n_batch, t5, n).transpose(2, 1, 3, 0, 4).reshape(n_batch, lt, D_SSM)

        qk = _qkconv(proj, ml_conv_w[li], ml_conv_b[li], qk_scale, cols["mqk"], lc)
        gsel = gates[:, :, :4 * h].reshape(n_batch, lt, 4, h)
        gates_col = gsel.transpose(0, 3, 1, 2)
        gates_row = gsel.transpose(0, 3, 2, 1)
        gb = ml_gate_b[li].astype(F32)
        hb = _mlstm(qk, proj, gates_col, gates_row, gb.T.reshape(h, 1, 4), gb.T.reshape(h, 4, 1),
                    ml_norm_g[li], cols["mv"], cols["mo"], lc)

        qg = jnp.tile(da_qnorm_g[li].astype(F32), D_DIFF // DIFF_QK_HEAD).reshape(1, D_DIFF)
        kg = jnp.tile(da_knorm_g[li].astype(F32), D_DIFF // DIFF_QK_HEAD).reshape(1, D_DIFF)
        qn, kn = _qkprep(proj, cos_t, sin_t, qg, kg, ones_bd, cols["dq"], cols["dk"])
        lam_init = 0.8 - 0.6 * math.exp(-0.3 * li)
        scal = jnp.array([lam_init, 1.0 - lam_init], F32)
        oc = _attention(scal, qn, kn, proj, da_lambda[li].astype(F32), da_subln_g[li].astype(F32), cols["dv"], lc)

        x_all = _merge(x_all, mods3, y5, hb, oc, proj, ssm_glu_w[li].astype(BF16),
                       ssm_glu_b[li].astype(F32).reshape(1, -1), w_ssm_out[li].astype(BF16),
                       w_ml_out[li].astype(BF16), w_da_out[li].astype(BF16), w_out[li].astype(BF16),
                       cols, n_batch, lc, skip_ctx=last)
    return x_all
```

```python
import functools
import math

import jax
import jax.numpy as jnp
from jax import lax
from jax.experimental import pallas as pl
from jax.experimental.pallas import tpu as pltpu

F32 = jnp.float32
BF16 = jnp.bfloat16
HIGHEST = lax.Precision.HIGHEST

EPS = 1e-6
GRID_W = 64
ROPE_BASE = 10000.0
LANES = 128

D_SSM = 512
SSM_GROUP = 16
N_SSM_GROUPS = D_SSM // SSM_GROUP
SSM_STATE = 64
S5_CHUNK = 16
S5_K = S5_CHUNK * SSM_GROUP
S5_LANE_GROUPS = LANES // SSM_GROUP
S5_SLABS = D_SSM // LANES
S5_WIDE = S5_CHUNK * LANES
S5_ROW_TILES = (768, 512, 256)
D_MLSTM = 512
N_MLSTM_HEADS = 4
MLSTM_HEAD = D_MLSTM // N_MLSTM_HEADS
MLSTM_CHUNK = 128
QK_CONV = 3
N_GATES = 4 * N_MLSTM_HEADS
N_DIFF_HEADS = 4
DIFF_QK_HEAD = 64
DIFF_V_HEAD = 2 * DIFF_QK_HEAD
D_DIFF = N_DIFF_HEADS * DIFF_V_HEAD
ATTN_KEY_CHUNK = 768
N_BRANCH = 3

ROW_TILE = 256
VMEM_LIMIT = 56 * 1024 * 1024


def _cols(d_model):
    off = {"gl": 0}
    o = N_BRANCH * d_model
    for name, w in (("sz", D_SSM), ("mqk", 2 * D_MLSTM), ("mv", D_MLSTM), ("mo", D_MLSTM),
                    ("mz", D_MLSTM), ("dq", D_DIFF), ("dk", D_DIFF), ("dv", D_DIFF), ("dz", D_DIFF)):
        off[name] = o
        o += w
    off["total"] = o
    return off


def _cparams(sem):
    return pltpu.CompilerParams(dimension_semantics=sem, vmem_limit_bytes=VMEM_LIMIT)


def _nt_dot(a, b):
    return lax.dot_general(a, b, (((1,), (1,)), ((), ())), preferred_element_type=F32)


def _tn_dot(a, b):
    return lax.dot_general(a, b, (((0,), (0,)), ((), ())), preferred_element_type=F32)


def _dot(a, b):
    return jnp.dot(a, b, preferred_element_type=F32)


def _dot_split(x, w):
    hi = x.astype(BF16)
    lo = (x - hi.astype(F32)).astype(BF16)
    return _dot(hi, w) + _dot(lo, w)


def _mods_kernel(c_ref, w_ref, b_ref, o_ref):
    c = c_ref[...]
    act = c * jax.nn.sigmoid(c)
    o_ref[0] = jnp.dot(act, w_ref[0], preferred_element_type=F32, precision=HIGHEST) + b_ref[0]


def _modulation(cvec, ada_w, ada_b):
    depth, d, d3 = ada_w.shape
    rows = cvec.shape[0]
    return pl.pallas_call(
        _mods_kernel,
        grid=(depth, d3 // d),
        in_specs=[pl.BlockSpec((rows, d), lambda l, j: (0, 0)),
                  pl.BlockSpec((1, d, d), lambda l, j: (l, 0, j)),
                  pl.BlockSpec((1, 1, d), lambda l, j: (l, 0, j))],
        out_specs=pl.BlockSpec((1, rows, d), lambda l, j: (l, 0, j)),
        out_shape=jax.ShapeDtypeStruct((depth, rows, d3), F32),
        compiler_params=_cparams(("arbitrary", "arbitrary")),
        name="adaln_modulation",
    )(cvec, ada_w, ada_b.reshape(depth, 1, d3))


def _inproj_kernel(x_ref, mod_ref, g_ref, w_ref, wsu_ref, wgt_ref, o_ref, su_ref, ogt_ref, *, d_model, col_chunk):
    x = x_ref[0]
    ms = jnp.mean(x * x, axis=-1, keepdims=True)
    y = x * lax.rsqrt(ms + EPS) * g_ref[...]
    sh = mod_ref[0, :, 0:d_model]
    sc = mod_ref[0, :, d_model:2 * d_model]
    hb = (y * (1.0 + sc) + sh).astype(BF16)
    n_out = o_ref.shape[-1]
    for c0 in range(0, n_out, col_chunk):
        o_ref[0, :, c0:c0 + col_chunk] = _dot(hb, w_ref[:, c0:c0 + col_chunk]).astype(BF16)
    rows = x.shape[0]
    for j in range(S5_SLABS):
        su = _dot(hb, wsu_ref[:, j * LANES:(j + 1) * LANES]).astype(BF16)
        su_ref[j, :, 0] = su.reshape(rows // S5_CHUNK, S5_CHUNK, LANES)
    ogt_ref[0] = _nt_dot(wgt_ref[...], hb)


def _inproj(x_all, mods3, norm_g, w_p, w_su, w_gt, n_batch, lc):
    b, lt, d = x_all.shape
    n_out = w_p.shape[1]
    ctx_tiles = lc // ROW_TILE
    kern = functools.partial(_inproj_kernel, d_model=d, col_chunk=512)
    blk_rows = ROW_TILE // S5_CHUNK
    return pl.pallas_call(
        kern,
        grid=(b, lt // ROW_TILE),
        in_specs=[pl.BlockSpec((1, ROW_TILE, d), lambda bi, i: (bi, i, 0)),
                  pl.BlockSpec((1, 1, 3 * d), lambda bi, i: (jnp.where(i < ctx_tiles, n_batch, bi), 0, 0)),
                  pl.BlockSpec((1, d), lambda bi, i: (0, 0)),
                  pl.BlockSpec((d, n_out), lambda bi, i: (0, 0), pipeline_mode=pl.Buffered(1)),
                  pl.BlockSpec((d, D_SSM), lambda bi, i: (0, 0)),
                  pl.BlockSpec((N_GATES, d), lambda bi, i: (0, 0))],
        out_specs=[pl.BlockSpec((1, ROW_TILE, n_out), lambda bi, i: (bi, i, 0)),
                   pl.BlockSpec((S5_SLABS, blk_rows, 1, S5_CHUNK, LANES), lambda bi, i: (0, i, bi, 0, 0)),
                   pl.BlockSpec((1, N_GATES, ROW_TILE), lambda bi, i: (bi, 0, i))],
        out_shape=[jax.ShapeDtypeStruct((b, lt, n_out), BF16),
                   jax.ShapeDtypeStruct((S5_SLABS, lt // S5_CHUNK, b, S5_CHUNK, LANES), BF16),
                   jax.ShapeDtypeStruct((b, N_GATES, lt), F32)],
        compiler_params=_cparams(("parallel", "parallel")),
        name="adaln_inproj",
    )(x_all, mods3, norm_g.reshape(1, d), w_p, w_su, w_gt)


def _s5_operators(lam_re, lam_im, log_step, b_re, b_im, c_re, c_im, d_skip):
    t = S5_CHUNK
    g, p, n = N_SSM_GROUPS, SSM_STATE, SSM_GROUP
    hp = HIGHEST
    tin = jnp.zeros((g, t, n, t, n), F32)
    m1_states, m2_rows, decs = [], [], []
    s_idx = jnp.arange(t)
    for d in range(2):
        lre = jnp.minimum(lam_re[d].astype(F32), -1e-4)
        lim = lam_im[d].astype(F32)
        dt = jnp.exp(log_step[d].astype(F32))[:, None]
        ld_re, ld_im = lre * dt, lim * dt
        mag = jnp.exp(ld_re)
        ab_re, ab_im = mag * jnp.cos(ld_im), mag * jnp.sin(ld_im)
        den = lre * lre + lim * lim
        num_re, num_im = ab_re - 1.0, ab_im
        coef_re = (num_re * lre + num_im * lim) / den
        coef_im = (num_im * lre - num_re * lim) / den
        bre, bim = b_re[d].astype(F32), b_im[d].astype(F32)
        bb_re = coef_re[..., None] * bre - coef_im[..., None] * bim
        bb_im = coef_re[..., None] * bim + coef_im[..., None] * bre
        cre, cim = c_re[d].astype(F32), c_im[d].astype(F32)
        k_pow = jnp.arange(t + 1, dtype=F32)[:, None, None]
        pmag = jnp.exp(k_pow * ld_re)
        pw_re, pw_im = pmag * jnp.cos(k_pow * ld_im), pmag * jnp.sin(k_pow * ld_im)
        ab_b_re = pw_re[..., None] * bb_re - pw_im[..., None] * bb_im
        ab_b_im = pw_re[..., None] * bb_im + pw_im[..., None] * bb_re
        kern = (jnp.einsum("gop,kgpn->kgon", cre, ab_b_re, precision=hp)
                - jnp.einsum("gop,kgpn->kgon", cim, ab_b_im, precision=hp))
        lag = (s_idx[None, :] - s_idx[:, None]) if d == 0 else (s_idx[:, None] - s_idx[None, :])
        valid = lag >= 0
        kk = kern[jnp.clip(lag, 0, t)]
        kk = jnp.where(valid[:, :, None, None, None], kk, 0.0)
        tin = tin + jnp.transpose(kk, (2, 0, 4, 1, 3))
        pw_idx = (t - 1 - s_idx) if d == 0 else s_idx
        st_re = jnp.transpose(ab_b_re[pw_idx], (1, 0, 3, 2))
        st_im = jnp.transpose(ab_b_im[pw_idx], (1, 0, 3, 2))
        m1_states.append(jnp.concatenate([st_re, st_im], -1).reshape(g, t * n, 2 * p))
        m1_states.append(jnp.concatenate([st_im, st_re], -1).reshape(g, t * n, 2 * p))
        rp = (s_idx + 1) if d == 0 else (t - s_idx)
        ca_re = cre[None] * pw_re[rp][:, :, None, :] - cim[None] * pw_im[rp][:, :, None, :]
        ca_im = cre[None] * pw_im[rp][:, :, None, :] + cim[None] * pw_re[rp][:, :, None, :]
        rd = jnp.concatenate([jnp.transpose(ca_re, (1, 3, 0, 2)), -jnp.transpose(ca_im, (1, 3, 0, 2))], 1)
        m2_rows.append(rd.reshape(g, 2 * p, t * n))
        decs.append(jnp.concatenate([pw_re[t], pw_re[t]], -1))
        decs.append(jnp.concatenate([-pw_im[t], pw_im[t]], -1))
    eye = jnp.eye(t * n, dtype=F32).reshape(1, t, n, t, n)
    tin = tin + eye * d_skip.astype(F32).reshape(g, 1, n, 1, 1)
    m1 = jnp.concatenate([tin.reshape(g, t * n, t * n)] + m1_states, axis=-1)
    m2 = jnp.concatenate(m2_rows, axis=1)
    dec = jnp.stack(decs, axis=1)
    return m1.astype(BF16), m2.astype(BF16), dec


def _s5_lane_permutation():
    src = jnp.arange(S5_WIDE)
    s, rem = src // LANES, src % LANES
    g8, n = rem // SSM_GROUP, rem % SSM_GROUP
    dst = g8 * S5_K + s * SSM_GROUP + n
    return (dst[:, None] == jnp.arange(S5_WIDE)[None, :]).astype(BF16)


def _s5_gather_kernel(l_ref, p_ref, o_ref):
    res = _dot(l_ref[0], p_ref[...])
    for g8 in range(S5_LANE_GROUPS):
        o_ref[g8] = res[:, g8 * S5_K:(g8 + 1) * S5_K].astype(BF16)


def _s5_row_tile(n_rows):
    return next(t for t in S5_ROW_TILES if n_rows % t == 0)


def _s5_gather(su_wide, perm):
    slabs, r, wide = su_wide.shape
    rt = _s5_row_tile(r)
    return pl.pallas_call(
        _s5_gather_kernel,
        grid=(slabs, r // rt),
        in_specs=[pl.BlockSpec((1, rt, wide), lambda j, i: (j, i, 0)),
                  pl.BlockSpec((wide, wide), lambda j, i: (0, 0), pipeline_mode=pl.Buffered(1))],
        out_specs=pl.BlockSpec((S5_LANE_GROUPS, rt, S5_K), lambda j, i: (j, i, 0)),
        out_shape=jax.ShapeDtypeStruct((slabs * S5_LANE_GROUPS, r, S5_K), BF16),
        compiler_params=_cparams(("parallel", "parallel")),
        name="s5_gather",
    )(su_wide, perm)


def _s5_scatter_kernel(y_ref, p_ref, o_ref):
    y = jnp.concatenate([y_ref[g8] for g8 in range(S5_LANE_GROUPS)], axis=1)
    o_ref[0] = _nt_dot(y, p_ref[...]).astype(BF16)


def _s5_scatter(y_blocks, perm):
    g, r, k = y_blocks.shape
    slabs = g // S5_LANE_GROUPS
    rt = _s5_row_tile(r)
    return pl.pallas_call(
        _s5_scatter_kernel,
        grid=(slabs, r // rt),
        in_specs=[pl.BlockSpec((S5_LANE_GROUPS, rt, k), lambda j, i: (j, i, 0)),
                  pl.BlockSpec((S5_WIDE, S5_WIDE), lambda j, i: (0, 0), pipeline_mode=pl.Buffered(1))],
        out_specs=pl.BlockSpec((1, rt, S5_WIDE), lambda j, i: (j, i, 0)),
        out_shape=jax.ShapeDtypeStruct((slabs, r, S5_WIDE), BF16),
        compiler_params=_cparams(("parallel", "parallel")),
        name="s5_scatter",
    )(y_blocks, perm)


def _s5_kernel(x_ref, m1_ref, m2_ref, dec_ref, o_ref, p_ref, h_ref, *, n_batch, n_ctx_chunks, n_chunks):
    k, sw = S5_K, 2 * SSM_STATE
    p_ref[...] = _dot(x_ref[0], m1_ref[0])
    a1f, a2f = dec_ref[0, 0:1, :], dec_ref[0, 1:2, :]
    a1b, a2b = dec_ref[0, 2:3, :], dec_ref[0, 3:4, :]
    zero = jnp.zeros((n_batch, sw), F32)

    def step(j, carry):
        hf, hfs, hb, hbs = carry
        cb = jnp.where(j < n_ctx_chunks, n_ctx_chunks - 1 - j, n_chunks - 1 + n_ctx_chunks - j)
        rf = pl.multiple_of(j * n_batch, n_batch)
        rb = pl.multiple_of(cb * n_batch, n_batch)
        h_ref[pl.ds(rf, n_batch), 0:sw] = hf
        h_ref[pl.ds(rb, n_batch), sw:2 * sw] = hb
        sf = p_ref[pl.ds(rf, n_batch), k:k + sw]
        sfs = p_ref[pl.ds(rf, n_batch), k + sw:k + 2 * sw]
        sb = p_ref[pl.ds(rb, n_batch), k + 2 * sw:k + 3 * sw]
        sbs = p_ref[pl.ds(rb, n_batch), k + 3 * sw:k + 4 * sw]
        return (a1f * hf + a2f * hfs + sf, a1f * hfs - a2f * hf + sfs,
                a1b * hb + a2b * hbs + sb, a1b * hbs - a2b * hb + sbs)

    lax.fori_loop(0, n_chunks, step, (zero, zero, zero, zero))
    inter = _dot(h_ref[...].astype(BF16), m2_ref[0])
    o_ref[0] = (p_ref[:, 0:k] + inter).astype(BF16)


def _s5_mixer(x_blocks, m1, m2, dec, n_batch, n_ctx_chunks):
    g, r, k = x_blocks.shape
    n_chunks = r // n_batch
    kern = functools.partial(_s5_kernel, n_batch=n_batch, n_ctx_chunks=n_ctx_chunks, n_chunks=n_chunks)
    return pl.pallas_call(
        kern,
        grid=(g,),
        in_specs=[pl.BlockSpec((1, r, k), lambda i: (i, 0, 0)),
                  pl.BlockSpec((1, k, m1.shape[2]), lambda i: (i, 0, 0)),
                  pl.BlockSpec((1, m2.shape[1], k), lambda i: (i, 0, 0)),
                  pl.BlockSpec((1, 4, 2 * SSM_STATE), lambda i: (i, 0, 0))],
        out_specs=pl.BlockSpec((1, r, k), lambda i: (i, 0, 0)),
        out_shape=jax.ShapeDtypeStruct((g, r, k), BF16),
        scratch_shapes=[pltpu.VMEM((r, m1.shape[2]), F32), pltpu.VMEM((r, 4 * SSM_STATE), F32)],
        compiler_params=_cparams(("parallel",)),
        name="s5_mixer",
    )(x_blocks, m1, m2, dec)


def _qkconv_kernel(x_ref, w_ref, b_ref, s_ref, o_ref, *, lc):
    x = x_ref[0].astype(F32)
    lt = x.shape[0]
    row = lax.broadcasted_iota(jnp.int32, (lt, 1), 0)
    prev = jnp.where((row == 0) | (row == lc), 0.0, pltpu.roll(x, 1, axis=0))
    nxt = jnp.where((row == lc - 1) | (row == lt - 1), 0.0, pltpu.roll(x, lt - 1, axis=0))
    y = w_ref[0:1, :] * prev + w_ref[1:2, :] * x + w_ref[2:3, :] * nxt + b_ref[...]
    o_ref[0] = (y * jax.nn.sigmoid(y) * s_ref[...]).astype(BF16)


def _qkconv(proj, conv_w, conv_b, post_scale, col0, lc):
    b, lt, _ = proj.shape
    width = conv_w.shape[1]
    cw = 256
    blk0 = col0 // cw
    kern = functools.partial(_qkconv_kernel, lc=lc)
    return pl.pallas_call(
        kern,
        grid=(b, width // cw),
        in_specs=[pl.BlockSpec((1, lt, cw), lambda bi, j: (bi, 0, blk0 + j)),
                  pl.BlockSpec((QK_CONV, cw), lambda bi, j: (0, j)),
                  pl.BlockSpec((1, cw), lambda bi, j: (0, j)),
                  pl.BlockSpec((1, cw), lambda bi, j: (0, j))],
        out_specs=pl.BlockSpec((1, lt, cw), lambda bi, j: (bi, 0, j)),
        out_shape=jax.ShapeDtypeStruct((b, lt, width), BF16),
        compiler_params=_cparams(("parallel", "parallel")),
        name="mlstm_qk_conv",
    )(proj, conv_w, conv_b.reshape(1, width), post_scale)


def _log_sigmoid(x):
    return jnp.minimum(x, 0.0) - jnp.log1p(jnp.exp(-jnp.abs(x)))


def _mlstm_chunk(q, k, v, ig, lf, f_row, cn0, m0, rev):
    t, dh = MLSTM_CHUNK, MLSTM_HEAD
    r_i = lax.broadcasted_iota(jnp.int32, (t, t), 0)
    c_i = lax.broadcasted_iota(jnp.int32, (t, t), 1)
    tri = (c_i >= r_i) if rev else (c_i <= r_i)
    ones = jnp.ones((t, dh), BF16)
    f_col = _dot_split(jnp.where(tri, lf, 0.0), ones)
    i_col = _dot_split(jnp.where(c_i == r_i, ig, 0.0), ones)
    r_row = f_row - ig
    f_end = f_row[:, 0:1] if rev else f_row[:, t - 1:t]
    log_d = jnp.where(tri, f_col - r_row, -jnp.inf)
    log_inter = f_col + m0
    m_t = jnp.maximum(jnp.max(log_d, axis=1, keepdims=True), log_inter)
    s_qk = (_nt_dot(q, k) * jnp.exp(log_d - m_t)).astype(BF16)
    e_inter = jnp.exp(log_inter - m_t)
    intra = _dot(s_qk, jnp.concatenate([v, ones], axis=1))
    inter = _dot(q, cn0.astype(BF16))
    num = intra[:, 0:dh] + e_inter * inter[:, 0:dh]
    den = intra[:, dh:2 * dh] + e_inter * inter[:, dh:2 * dh]
    h = num / jnp.maximum(jnp.abs(den), jnp.exp(-m_t))
    m_loc = f_end - jnp.min(r_row, axis=1, keepdims=True)
    e_end = jnp.exp(f_end - (f_col - i_col) - m_loc)
    w = jnp.concatenate([e_end * v.astype(F32), e_end], axis=1).astype(BF16)
    d_cn = _tn_dot(k, w)
    m_new = jnp.maximum(f_end + m0, m_loc)
    a = jnp.exp(f_end + m0 - m_new)
    bb = jnp.exp(m_loc - m_new)
    return h, a * cn0 + bb * d_cn, m_new


def _mlstm_kernel(q_ref, k_ref, v_ref, o_ref, gt_ref, gb_ref, g_ref, out_ref, acc_ref, cn_ref, m_ref,
                  *, n_ctx_chunks, n_chunks):
    t, dh, nh = MLSTM_CHUNK, MLSTM_HEAD, N_MLSTM_HEADS
    acc_ref[...] = jnp.zeros_like(acc_ref)
    cn_ref[...] = jnp.zeros_like(cn_ref)
    m_ref[...] = jnp.zeros_like(m_ref)

    def step(j, carry):
        r_i = lax.broadcasted_iota(jnp.int32, (t, t), 0)
        c_i = lax.broadcasted_iota(jnp.int32, (t, t), 1)
        for d in range(2):
            rev = d == 1
            if rev:
                c = jnp.where(j < n_ctx_chunks, n_ctx_chunks - 1 - j, n_chunks - 1 + n_ctx_chunks - j)
            else:
                c = j
            r0 = pl.multiple_of(c * t, t)
            gates = gt_ref[0, :, pl.ds(r0, t)] + gb_ref[...]
            ig = gates[2 * nh * d:2 * nh * d + nh]
            lf = _log_sigmoid(gates[2 * nh * d + nh:2 * nh * (d + 1)])
            cum = ((r_i >= c_i) if rev else (r_i <= c_i)).astype(BF16)
            f_rows = _dot_split(lf, cum)
            for hd in range(nh):
                idx = d * nh + hd
                cs = slice(hd * dh, (hd + 1) * dh)
                h, cn, m = _mlstm_chunk(q_ref[0, pl.ds(r0, t), cs], k_ref[0, pl.ds(r0, t), cs],
                                        v_ref[0, pl.ds(r0, t), cs], ig[hd:hd + 1], lf[hd:hd + 1],
                                        f_rows[hd:hd + 1], cn_ref[idx], m_ref[idx:idx + 1, 0:1], rev)
                acc_ref[pl.ds(r0, t), cs] += h
                cn_ref[idx] = cn
                m_ref[idx:idx + 1, :] = jnp.broadcast_to(m, (1, LANES))
        return carry

    lax.fori_loop(0, n_chunks, step, 0)

    def finish(c, carry):
        r0 = pl.multiple_of(c * t, t)
        for hd in range(nh):
            cs = slice(hd * dh, (hd + 1) * dh)
            hsum = acc_ref[pl.ds(r0, t), cs]
            y = hsum * lax.rsqrt(jnp.mean(hsum * hsum, axis=-1, keepdims=True) + EPS) * g_ref[:, cs]
            out_ref[0, pl.ds(r0, t), cs] = (y * jax.nn.sigmoid(o_ref[0, pl.ds(r0, t), cs].astype(F32))).astype(BF16)
        return carry

    lax.fori_loop(0, n_chunks, finish, 0)


def _mlstm(qk, proj, gates_t, gate_b, norm_g, col_v, col_o, lc):
    b, lt, _ = qk.shape
    w, t = D_MLSTM, MLSTM_CHUNK
    kern = functools.partial(_mlstm_kernel, n_ctx_chunks=lc // t, n_chunks=lt // t)
    return pl.pallas_call(
        kern,
        grid=(b,),
        in_specs=[pl.BlockSpec((1, lt, w), lambda bi: (bi, 0, 0)),
                  pl.BlockSpec((1, lt, w), lambda bi: (bi, 0, 1)),
                  pl.BlockSpec((1, lt, w), lambda bi: (bi, 0, col_v // w)),
                  pl.BlockSpec((1, lt, w), lambda bi: (bi, 0, col_o // w)),
                  pl.BlockSpec((1, N_GATES, lt), lambda bi: (bi, 0, 0)),
                  pl.BlockSpec((N_GATES, 1), lambda bi: (0, 0)),
                  pl.BlockSpec((1, w), lambda bi: (0, 0))],
        out_specs=pl.BlockSpec((1, lt, w), lambda bi: (bi, 0, 0)),
        out_shape=jax.ShapeDtypeStruct((b, lt, w), BF16),
        scratch_shapes=[pltpu.VMEM((lt, w), F32),
                        pltpu.VMEM((2 * N_MLSTM_HEADS, MLSTM_HEAD, 2 * MLSTM_HEAD), F32),
                        pltpu.VMEM((2 * N_MLSTM_HEADS, LANES), F32)],
        compiler_params=_cparams(("parallel",)),
        name="mlstm_mixer",
    )(qk, qk, proj, proj, gates_t, gate_b, norm_g.reshape(1, w))


def _segment_mean_sq(x, ones_bd, seg):
    return _dot_split(x * x, ones_bd) * (1.0 / seg)


def _qkprep_kernel(q_ref, k_ref, cos_ref, sin_ref, qg_ref, kg_ref, ones_ref, qo_ref, ko_ref):
    cos, sin = cos_ref[...], sin_ref[...]
    lane = lax.broadcasted_iota(jnp.int32, (1, LANES), 1)
    first_half = (lane // 16) % 2 == 0

    def prep(ref, g_ref, scale):
        x = ref[0].astype(F32)
        y = x * lax.rsqrt(_segment_mean_sq(x, ones_ref[...], DIFF_QK_HEAD) + EPS) * g_ref[...]
        outs = []
        for c0 in range(0, y.shape[1], LANES):
            yb = y[:, c0:c0 + LANES]
            partner = jnp.where(first_half, pltpu.roll(yb, LANES - 16, axis=1), pltpu.roll(yb, 16, axis=1))
            outs.append(yb * cos + partner * sin)
        return (jnp.concatenate(outs, axis=1) * scale).astype(BF16)

    qo_ref[0] = prep(q_ref, qg_ref, DIFF_QK_HEAD ** -0.5)
    ko_ref[0] = prep(k_ref, kg_ref, 1.0)


def _qkprep(proj, cos_t, sin_t, qg, kg, ones_bd, col_q, col_k):
    b, lt, _ = proj.shape
    w = D_DIFF
    qb, kb = col_q // w, col_k // w
    return pl.pallas_call(
        _qkprep_kernel,
        grid=(b, lt // ROW_TILE),
        in_specs=[pl.BlockSpec((1, ROW_TILE, w), lambda bi, i: (bi, i, qb)),
                  pl.BlockSpec((1, ROW_TILE, w), lambda bi, i: (bi, i, kb)),
                  pl.BlockSpec((ROW_TILE, LANES), lambda bi, i: (i, 0)),
                  pl.BlockSpec((ROW_TILE, LANES), lambda bi, i: (i, 0)),
                  pl.BlockSpec((1, w), lambda bi, i: (0, 0)),
                  pl.BlockSpec((1, w), lambda bi, i: (0, 0)),
                  pl.BlockSpec((w, w), lambda bi, i: (0, 0))],
        out_specs=[pl.BlockSpec((1, ROW_TILE, w), lambda bi, i: (bi, i, 0)),
                   pl.BlockSpec((1, ROW_TILE, w), lambda bi, i: (bi, i, 0))],
        out_shape=[jax.ShapeDtypeStruct((b, lt, w), BF16), jax.ShapeDtypeStruct((b, lt, w), BF16)],
        compiler_params=_cparams(("parallel", "parallel")),
        name="diffattn_qk_prep",
    )(proj, proj, cos_t, sin_t, qg, kg, ones_bd)


def _attn_kernel(sc_ref, q_ref, k_ref, v_ref, lam_ref, g_ref, o_ref, vext_ref, *, lc, ctx_tiles):
    i = pl.program_id(2)
    dv = DIFF_V_HEAD
    lp = lam_ref[...]
    lam = (jnp.exp(jnp.sum(lp[0:1] * lp[1:2], axis=1, keepdims=True))
           - jnp.exp(jnp.sum(lp[2:3] * lp[3:4], axis=1, keepdims=True)) + sc_ref[0])
    lane = lax.broadcasted_iota(jnp.int32, (1, dv), 1)

    @pl.when(i == 0)
    def _():
        vext_ref[:, 0:dv] = v_ref[0]
        vext_ref[:, dv:2 * dv] = jnp.ones((v_ref.shape[1], dv), BF16)

    def attend(nk, kc):
        q = q_ref[0]
        outs = []
        for c in range(2):
            keep = (lane < DIFF_QK_HEAD) if c == 0 else (lane >= DIFF_QK_HEAD)
            qc = jnp.where(keep, q, jnp.zeros_like(q))
            m_run, acc = None, None
            for k0 in range(0, nk, kc):
                s = _nt_dot(qc, k_ref[0, k0:k0 + kc, :])
                m_j = jnp.max(s, axis=1, keepdims=True)
                m_new = m_j if m_run is None else jnp.maximum(m_run, m_j)
                pv = _dot(jnp.exp(s - m_new).astype(BF16), vext_ref[k0:k0 + kc, :])
                acc = pv if m_run is None else acc * jnp.exp(m_run - m_new) + pv
                m_run = m_new
            outs.append(acc[:, 0:dv] / acc[:, dv:2 * dv])
        o = outs[0] - lam * outs[1]
        y = o * lax.rsqrt(jnp.mean(o * o, axis=-1, keepdims=True) + EPS) * g_ref[...]
        o_ref[0] = (y * sc_ref[1]).astype(BF16)

    @pl.when(i < ctx_tiles)
    def _():
        attend(lc, lc)

    @pl.when(i >= ctx_tiles)
    def _():
        nk = k_ref.shape[1]
        attend(nk, ATTN_KEY_CHUNK if nk % ATTN_KEY_CHUNK == 0 else nk)


def _attention(scal, qn, kn, proj, lam_p, subln_g, col_v, lc):
    b, lt, _ = qn.shape
    h, dv = N_DIFF_HEADS, DIFF_V_HEAD
    vb = col_v // dv
    kern = functools.partial(_attn_kernel, lc=lc, ctx_tiles=lc // ROW_TILE)
    return pl.pallas_call(
        kern,
        grid=(b, h, lt // ROW_TILE),
        in_specs=[pl.BlockSpec(memory_space=pltpu.SMEM),
                  pl.BlockSpec((1, ROW_TILE, dv), lambda bi, hi, i: (bi, i, hi)),
                  pl.BlockSpec((1, lt, dv), lambda bi, hi, i: (bi, 0, hi)),
                  pl.BlockSpec((1, lt, dv), lambda bi, hi, i: (bi, 0, vb + hi)),
                  pl.BlockSpec((4, DIFF_QK_HEAD), lambda bi, hi, i: (0, 0)),
                  pl.BlockSpec((1, dv), lambda bi, hi, i: (0, 0))],
        out_specs=pl.BlockSpec((1, ROW_TILE, dv), lambda bi, hi, i: (bi, i, hi)),
        out_shape=jax.ShapeDtypeStruct((b, lt, h * dv), BF16),
        scratch_shapes=[pltpu.VMEM((lt, 2 * dv), BF16)],
        compiler_params=_cparams(("parallel", "parallel", "arbitrary")),
        name="diff_attention",
    )(scal, qn, kn, proj, lam_p, subln_g.reshape(1, dv))


def _silu(x):
    return x * jax.nn.sigmoid(x)


def _merge_kernel(x_ref, mod_ref, y5_ref, sz_ref, hb_ref, mz_ref, oc_ref, dz_ref, gl_ref,
                  gw_ref, gb_ref, wa_ref, wb_ref, wc_ref, wo_ref, o_ref, *, d_model):
    f = lambda r: r[0].astype(F32)
    rows = x_ref.shape[1]
    y = jnp.concatenate([y5_ref[j, :, 0].reshape(rows, LANES) for j in range(S5_SLABS)], axis=1).astype(F32)
    ge = 0.5 * y * (1.0 + jnp.tanh(math.sqrt(2.0 / math.pi) * (y + 0.044715 * (y * y * y))))
    t = _dot(ge.astype(BF16), gw_ref[...]) + gb_ref[...]
    ya = t[:, 0:D_SSM] * jax.nn.sigmoid(t[:, D_SSM:2 * D_SSM])
    pa = _dot((ya * _silu(f(sz_ref))).astype(BF16), wa_ref[...])
    pb = _dot((f(hb_ref) * _silu(f(mz_ref))).astype(BF16), wb_ref[...])
    pc = _dot((f(oc_ref) * _silu(f(dz_ref))).astype(BF16), wc_ref[...])
    gates = jax.nn.sigmoid(f(gl_ref))
    m = (gates[:, 0:d_model] * pa + gates[:, d_model:2 * d_model] * pb
         + gates[:, 2 * d_model:3 * d_model] * pc)
    out = _dot(m.astype(BF16), wo_ref[...])
    o_ref[0] = x_ref[0] + mod_ref[0, :, 2 * d_model:3 * d_model] * out


def _merge(x_all, mods3, y5, hb, oc, proj, glu_w, glu_b, wa, wb, wc, wo, cols, n_batch, lc, skip_ctx):
    b, lt, d = x_all.shape
    ctx_tiles = lc // ROW_TILE
    t0 = ctx_tiles if skip_ctx else 0
    w = D_SSM
    blk = lambda name: cols[name] // w
    blk_rows = ROW_TILE // S5_CHUNK
    tok = lambda j: pl.BlockSpec((1, ROW_TILE, w), lambda bi, i: (bi, i + t0, j))
    full = lambda a: pl.BlockSpec(a.shape, lambda bi, i: (0,) * a.ndim)
    kern = functools.partial(_merge_kernel, d_model=d)
    return pl.pallas_call(
        kern,
        grid=(b, lt // ROW_TILE - t0),
        in_specs=[pl.BlockSpec((1, ROW_TILE, d), lambda bi, i: (bi, i + t0, 0)),
                  pl.BlockSpec((1, 1, 3 * d), lambda bi, i: (jnp.where(i + t0 < ctx_tiles, n_batch, bi), 0, 0)),
                  pl.BlockSpec((S5_SLABS, blk_rows, 1, S5_CHUNK, LANES), lambda bi, i: (0, i + t0, bi, 0, 0)),
                  tok(blk("sz")), tok(0), tok(blk("mz")), tok(0), tok(blk("dz")),
                  pl.BlockSpec((1, ROW_TILE, N_BRANCH * d), lambda bi, i: (bi, i + t0, 0)),
                  full(glu_w), full(glu_b), full(wa), full(wb), full(wc), full(wo)],
        out_specs=pl.BlockSpec((1, ROW_TILE, d), lambda bi, i: (bi, i, 0)),
        out_shape=jax.ShapeDtypeStruct((b, lt - t0 * ROW_TILE, d), F32),
        compiler_params=_cparams(("parallel", "parallel")),
        name="merge_outproj",
    )(x_all, mods3, y5, proj, hb, proj, oc, proj, proj, glu_w, glu_b, wa, wb, wc, wo)


def _rope_tables(lc, l):
    half = DIFF_QK_HEAD // 2
    rows = l // GRID_W
    row = jnp.repeat(jnp.arange(rows), GRID_W).astype(F32)
    col = jnp.tile(jnp.arange(GRID_W), rows).astype(F32)
    inv = jnp.power(ROPE_BASE, -jnp.arange(0, half, 2, dtype=F32) / half)
    ang_r, ang_c = row[:, None] * inv, col[:, None] * inv
    cos64 = jnp.concatenate([jnp.cos(ang_r), jnp.cos(ang_r), jnp.cos(ang_c), jnp.cos(ang_c)], axis=1)
    sin64 = jnp.concatenate([-jnp.sin(ang_r), jnp.sin(ang_r), -jnp.sin(ang_c), jnp.sin(ang_c)], axis=1)
    cos_t = jnp.concatenate([jnp.ones((lc, DIFF_QK_HEAD), F32), cos64], axis=0)
    sin_t = jnp.concatenate([jnp.zeros((lc, DIFF_QK_HEAD), F32), sin64], axis=0)
    return jnp.tile(cos_t, (1, 2)), jnp.tile(sin_t, (1, 2))


def kernel(x, c, ctx, c_ctx, norm_g, ada_w, ada_b, w_in, ssm_lam_re, ssm_lam_im, ssm_log_step, ssm_b_re, ssm_b_im, ssm_c_re, ssm_c_im, ssm_d, ssm_glu_w, ssm_glu_b, w_ssm_out, ml_conv_w, ml_conv_b, ml_gate_b, ml_norm_g, w_ml_out, da_qnorm_g, da_knorm_g, da_lambda, da_subln_g, w_da_out, w_out):
    n_batch, l, d = x.shape
    lc = ctx.shape[1]
    lt = lc + l
    depth = w_in.shape[0]
    cols = _cols(d)
    t5 = S5_CHUNK
    n_rows = (lt // t5) * n_batch
    assert lc % ROW_TILE == 0 and l % ROW_TILE == 0 and lc % MLSTM_CHUNK == 0 and n_batch % 8 == 0
    assert MLSTM_CHUNK == MLSTM_HEAD == LANES

    x_all = jnp.concatenate([ctx, x], axis=1)

    mod_rows = -(-(n_batch + 1) // 8) * 8
    cvec = jnp.zeros((mod_rows, d), F32).at[:n_batch].set(c).at[n_batch].set(c_ctx)
    mods = _modulation(cvec, ada_w, ada_b)

    g0 = 2 * D_SSM + 5 * D_MLSTM
    g1 = g0 + N_GATES
    m0 = g1 + 4 * D_DIFF
    w_p = jnp.concatenate([w_in[:, :, m0:], w_in[:, :, D_SSM:g0], w_in[:, :, g1:m0]], axis=2).astype(BF16)
    w_su = w_in[:, :, :D_SSM].astype(BF16)
    w_gt = jnp.swapaxes(w_in[:, :, g0:g1], 1, 2).astype(BF16)

    cos_t, sin_t = _rope_tables(lc, l)
    ones_bd = jnp.kron(jnp.eye(D_DIFF // DIFF_QK_HEAD, dtype=F32),
                       jnp.ones((DIFF_QK_HEAD, DIFF_QK_HEAD), F32)).astype(BF16)
    qk_scale = jnp.concatenate([jnp.ones((1, D_MLSTM), F32), jnp.full((1, D_MLSTM), MLSTM_HEAD ** -0.5, F32)], 1)
    perm = _s5_lane_permutation()
    s5_m1, s5_m2, s5_dec = jax.vmap(_s5_operators)(ssm_lam_re, ssm_lam_im, ssm_log_step, ssm_b_re, ssm_b_im,
                                                   ssm_c_re, ssm_c_im, ssm_d)

    for li in range(depth):
        last = li == depth - 1
        mods3 = mods[li].reshape(mod_rows, 1, 3 * d)
        proj, su, gates_t = _inproj(x_all, mods3, norm_g[li], w_p[li], w_su[li], w_gt[li], n_batch, lc)

        xb = _s5_gather(su.reshape(S5_SLABS, n_rows, S5_WIDE), perm)
        yb = _s5_mixer(xb, s5_m1[li], s5_m2[li], s5_dec[li], n_batch, lc // t5)
        y5 = _s5_scatter(yb, perm).reshape(su.shape)

        qk = _qkconv(proj, ml_conv_w[li], ml_conv_b[li], qk_scale, cols["mqk"], lc)
        hb = _mlstm(qk, proj, gates_t, ml_gate_b[li].astype(F32).reshape(N_GATES, 1), ml_norm_g[li],
                    cols["mv"], cols["mo"], lc)

        qg = jnp.tile(da_qnorm_g[li].astype(F32), D_DIFF // DIFF_QK_HEAD).reshape(1, D_DIFF)
        kg = jnp.tile(da_knorm_g[li].astype(F32), D_DIFF // DIFF_QK_HEAD).reshape(1, D_DIFF)
        qn, kn = _qkprep(proj, cos_t, sin_t, qg, kg, ones_bd, cols["dq"], cols["dk"])
        lam_init = 0.8 - 0.6 * math.exp(-0.3 * li)
        scal = jnp.array([lam_init, 1.0 - lam_init], F32)
        oc = _attention(scal, qn, kn, proj, da_lambda[li].astype(F32), da_subln_g[li].astype(F32), cols["dv"], lc)

        x_all = _merge(x_all, mods3, y5, hb, oc, proj, ssm_glu_w[li].astype(BF16),
                       ssm_glu_b[li].astype(F32).reshape(1, -1), w_ssm_out[li].astype(BF16),
                       w_ml_out[li].astype(BF16), w_da_out[li].astype(BF16), w_out[li].astype(BF16),
                       cols, n_batch, lc, skip_ctx=last)
    return x_all
```

```python
import functools
import math

import jax
import jax.numpy as jnp
from jax import lax
from jax.experimental import pallas as pl
from jax.experimental.pallas import tpu as pltpu

F32 = jnp.float32
BF16 = jnp.bfloat16
HIGHEST = lax.Precision.HIGHEST

EPS = 1e-6
GRID_W = 64
ROPE_BASE = 10000.0
LANES = 128

D_SSM = 512
SSM_GROUP = 16
N_SSM_GROUPS = D_SSM // SSM_GROUP
SSM_STATE = 64
S5_CHUNK = 16
S5_K = S5_CHUNK * SSM_GROUP
S5_LANE_GROUPS = LANES // SSM_GROUP
S5_SLABS = D_SSM // LANES
S5_WIDE = S5_CHUNK * LANES
S5_ROW_TILES = (768, 512, 256)
D_MLSTM = 512
N_MLSTM_HEADS = 4
MLSTM_HEAD = D_MLSTM // N_MLSTM_HEADS
MLSTM_CHUNK = 128
QK_CONV = 3
N_GATES = 4 * N_MLSTM_HEADS
N_DIFF_HEADS = 4
DIFF_QK_HEAD = 64
DIFF_V_HEAD = 2 * DIFF_QK_HEAD
D_DIFF = N_DIFF_HEADS * DIFF_V_HEAD
ATTN_KEY_CHUNK = 768
N_BRANCH = 3

ROW_TILE = 256
VMEM_LIMIT = 56 * 1024 * 1024


def _cols(d_model):
    off = {"gl": 0}
    o = N_BRANCH * d_model
    for name, w in (("sz", D_SSM), ("mqk", 2 * D_MLSTM), ("mv", D_MLSTM), ("mo", D_MLSTM),
                    ("mz", D_MLSTM), ("dq", D_DIFF), ("dk", D_DIFF), ("dv", D_DIFF), ("dz", D_DIFF)):
        off[name] = o
        o += w
    off["total"] = o
    return off


def _cparams(sem):
    return pltpu.CompilerParams(dimension_semantics=sem, vmem_limit_bytes=VMEM_LIMIT)


def _nt_dot(a, b):
    return lax.dot_general(a, b, (((1,), (1,)), ((), ())), preferred_element_type=F32)


def _tn_dot(a, b):
    return lax.dot_general(a, b, (((0,), (0,)), ((), ())), preferred_element_type=F32)


def _dot(a, b):
    return jnp.dot(a, b, preferred_element_type=F32)


def _dot_split(x, w):
    hi = x.astype(BF16)
    lo = (x - hi.astype(F32)).astype(BF16)
    return _dot(hi, w) + _dot(lo, w)


def _mods_kernel(c_ref, w_ref, b_ref, o_ref):
    c = c_ref[...]
    act = c * jax.nn.sigmoid(c)
    o_ref[0] = jnp.dot(act, w_ref[0], preferred_element_type=F32, precision=HIGHEST) + b_ref[0]


def _modulation(cvec, ada_w, ada_b):
    depth, d, d3 = ada_w.shape
    rows = cvec.shape[0]
    return pl.pallas_call(
        _mods_kernel,
        grid=(depth, d3 // d),
        in_specs=[pl.BlockSpec((rows, d), lambda l, j: (0, 0)),
                  pl.BlockSpec((1, d, d), lambda l, j: (l, 0, j)),
                  pl.BlockSpec((1, 1, d), lambda l, j: (l, 0, j))],
        out_specs=pl.BlockSpec((1, rows, d), lambda l, j: (l, 0, j)),
        out_shape=jax.ShapeDtypeStruct((depth, rows, d3), F32),
        compiler_params=_cparams(("arbitrary", "arbitrary")),
        name="adaln_modulation",
    )(cvec, ada_w, ada_b.reshape(depth, 1, d3))


def _qk_norm_rope(x, gain, cos, sin, ones_bd, scale):
    half = ones_bd.shape[0]
    ms = jnp.concatenate([_dot_split(x[:, c0:c0 + half] * x[:, c0:c0 + half], ones_bd)
                          for c0 in range(0, x.shape[1], half)], axis=1) * (1.0 / DIFF_QK_HEAD)
    y = x * lax.rsqrt(ms + EPS) * gain
    lane = lax.broadcasted_iota(jnp.int32, (1, LANES), 1)
    first_half = (lane // 16) % 2 == 0
    outs = []
    for c0 in range(0, y.shape[1], LANES):
        yb = y[:, c0:c0 + LANES]
        partner = jnp.where(first_half, pltpu.roll(yb, LANES - 16, axis=1), pltpu.roll(yb, 16, axis=1))
        outs.append(yb * cos + partner * sin)
    return jnp.concatenate(outs, axis=1) * scale


def _inproj_kernel(x_ref, mod_ref, g_ref, w_ref, wsu_ref, wgt_ref, cos_ref, sin_ref, qg_ref, kg_ref, ones_ref,
                   o_ref, su_ref, ogt_ref, *, d_model, col_chunk, col_q, col_k):
    x = x_ref[0]
    ms = jnp.mean(x * x, axis=-1, keepdims=True)
    y = x * lax.rsqrt(ms + EPS) * g_ref[...]
    sh = mod_ref[0, :, 0:d_model]
    sc = mod_ref[0, :, d_model:2 * d_model]
    hb = (y * (1.0 + sc) + sh).astype(BF16)
    n_out = o_ref.shape[-1]
    for c0 in range(0, n_out, col_chunk):
        acc = _dot(hb, w_ref[:, c0:c0 + col_chunk])
        if c0 == col_q:
            acc = _qk_norm_rope(acc, qg_ref[...], cos_ref[...], sin_ref[...], ones_ref[...], DIFF_QK_HEAD ** -0.5)
        elif c0 == col_k:
            acc = _qk_norm_rope(acc, kg_ref[...], cos_ref[...], sin_ref[...], ones_ref[...], 1.0)
        o_ref[0, :, c0:c0 + col_chunk] = acc.astype(BF16)
    rows = x.shape[0]
    for j in range(S5_SLABS):
        su = _dot(hb, wsu_ref[:, j * LANES:(j + 1) * LANES]).astype(BF16)
        su_ref[j, :, 0] = su.reshape(rows // S5_CHUNK, S5_CHUNK, LANES)
    ogt_ref[0] = _nt_dot(wgt_ref[...], hb)


def _inproj(x_all, mods3, norm_g, w_p, w_su, w_gt, cos_t, sin_t, qg, kg, ones_bd, cols, n_batch, lc):
    b, lt, d = x_all.shape
    n_out = w_p.shape[1]
    ctx_tiles = lc // ROW_TILE
    kern = functools.partial(_inproj_kernel, d_model=d, col_chunk=D_DIFF, col_q=cols["dq"], col_k=cols["dk"])
    assert cols["dq"] % D_DIFF == 0 and cols["dk"] % D_DIFF == 0
    blk_rows = ROW_TILE // S5_CHUNK
    return pl.pallas_call(
        kern,
        grid=(b, lt // ROW_TILE),
        in_specs=[pl.BlockSpec((1, ROW_TILE, d), lambda bi, i: (bi, i, 0)),
                  pl.BlockSpec((1, 1, 3 * d), lambda bi, i: (jnp.where(i < ctx_tiles, n_batch, bi), 0, 0)),
                  pl.BlockSpec((1, d), lambda bi, i: (0, 0)),
                  pl.BlockSpec((d, n_out), lambda bi, i: (0, 0), pipeline_mode=pl.Buffered(1)),
                  pl.BlockSpec((d, D_SSM), lambda bi, i: (0, 0)),
                  pl.BlockSpec((N_GATES, d), lambda bi, i: (0, 0)),
                  pl.BlockSpec((ROW_TILE, LANES), lambda bi, i: (i, 0)),
                  pl.BlockSpec((ROW_TILE, LANES), lambda bi, i: (i, 0)),
                  pl.BlockSpec((1, D_DIFF), lambda bi, i: (0, 0)),
                  pl.BlockSpec((1, D_DIFF), lambda bi, i: (0, 0)),
                  pl.BlockSpec(ones_bd.shape, lambda bi, i: (0, 0))],
        out_specs=[pl.BlockSpec((1, ROW_TILE, n_out), lambda bi, i: (bi, i, 0)),
                   pl.BlockSpec((S5_SLABS, blk_rows, 1, S5_CHUNK, LANES), lambda bi, i: (0, i, bi, 0, 0)),
                   pl.BlockSpec((1, N_GATES, ROW_TILE), lambda bi, i: (bi, 0, i))],
        out_shape=[jax.ShapeDtypeStruct((b, lt, n_out), BF16),
                   jax.ShapeDtypeStruct((S5_SLABS, lt // S5_CHUNK, b, S5_CHUNK, LANES), BF16),
                   jax.ShapeDtypeStruct((b, N_GATES, lt), F32)],
        compiler_params=_cparams(("parallel", "parallel")),
        name="adaln_inproj",
    )(x_all, mods3, norm_g.reshape(1, d), w_p, w_su, w_gt, cos_t, sin_t, qg, kg, ones_bd)


def _s5_operators(lam_re, lam_im, log_step, b_re, b_im, c_re, c_im, d_skip):
    t = S5_CHUNK
    g, p, n = N_SSM_GROUPS, SSM_STATE, SSM_GROUP
    hp = HIGHEST
    tin = jnp.zeros((g, t, n, t, n), F32)
    m1_states, m2_rows, decs = [], [], []
    s_idx = jnp.arange(t)
    for d in range(2):
        lre = jnp.minimum(lam_re[d].astype(F32), -1e-4)
        lim = lam_im[d].astype(F32)
        dt = jnp.exp(log_step[d].astype(F32))[:, None]
        ld_re, ld_im = lre * dt, lim * dt
        mag = jnp.exp(ld_re)
        ab_re, ab_im = mag * jnp.cos(ld_im), mag * jnp.sin(ld_im)
        den = lre * lre + lim * lim
        num_re, num_im = ab_re - 1.0, ab_im
        coef_re = (num_re * lre + num_im * lim) / den
        coef_im = (num_im * lre - num_re * lim) / den
        bre, bim = b_re[d].astype(F32), b_im[d].astype(F32)
        bb_re = coef_re[..., None] * bre - coef_im[..., None] * bim
        bb_im = coef_re[..., None] * bim + coef_im[..., None] * bre
        cre, cim = c_re[d].astype(F32), c_im[d].astype(F32)
        k_pow = jnp.arange(t + 1, dtype=F32)[:, None, None]
        pmag = jnp.exp(k_pow * ld_re)
        pw_re, pw_im = pmag * jnp.cos(k_pow * ld_im), pmag * jnp.sin(k_pow * ld_im)
        ab_b_re = pw_re[..., None] * bb_re - pw_im[..., None] * bb_im
        ab_b_im = pw_re[..., None] * bb_im + pw_im[..., None] * bb_re
        kern = (jnp.einsum("gop,kgpn->kgon", cre, ab_b_re, precision=hp)
                - jnp.einsum("gop,kgpn->kgon", cim, ab_b_im, precision=hp))
        lag = (s_idx[None, :] - s_idx[:, None]) if d == 0 else (s_idx[:, None] - s_idx[None, :])
        valid = lag >= 0
        kk = kern[jnp.clip(lag, 0, t)]
        kk = jnp.where(valid[:, :, None, None, None], kk, 0.0)
        tin = tin + jnp.transpose(kk, (2, 0, 4, 1, 3))
        pw_idx = (t - 1 - s_idx) if d == 0 else s_idx
        st_re = jnp.transpose(ab_b_re[pw_idx], (1, 0, 3, 2))
        st_im = jnp.transpose(ab_b_im[pw_idx], (1, 0, 3, 2))
        m1_states.append(jnp.concatenate([st_re, st_im], -1).reshape(g, t * n, 2 * p))
        m1_states.append(jnp.concatenate([st_im, st_re], -1).reshape(g, t * n, 2 * p))
        rp = (s_idx + 1) if d == 0 else (t - s_idx)
        ca_re = cre[None] * pw_re[rp][:, :, None, :] - cim[None] * pw_im[rp][:, :, None, :]
        ca_im = cre[None] * pw_im[rp][:, :, None, :] + cim[None] * pw_re[rp][:, :, None, :]
        rd = jnp.concatenate([jnp.transpose(ca_re, (1, 3, 0, 2)), -jnp.transpose(ca_im, (1, 3, 0, 2))], 1)
        m2_rows.append(rd.reshape(g, 2 * p, t * n))
        decs.append(jnp.concatenate([pw_re[t], pw_re[t]], -1))
        decs.append(jnp.concatenate([-pw_im[t], pw_im[t]], -1))
    eye = jnp.eye(t * n, dtype=F32).reshape(1, t, n, t, n)
    tin = tin + eye * d_skip.astype(F32).reshape(g, 1, n, 1, 1)
    m1 = jnp.concatenate([tin.reshape(g, t * n, t * n)] + m1_states, axis=-1)
    m2 = jnp.concatenate(m2_rows, axis=1)
    dec = jnp.stack(decs, axis=1)
    return m1.astype(BF16), m2.astype(BF16), dec


def _s5_lane_permutation():
    src = jnp.arange(S5_WIDE)
    s, rem = src // LANES, src % LANES
    g8, n = rem // SSM_GROUP, rem % SSM_GROUP
    dst = g8 * S5_K + s * SSM_GROUP + n
    return (dst[:, None] == jnp.arange(S5_WIDE)[None, :]).astype(BF16)


def _s5_gather_kernel(l_ref, p_ref, o_ref):
    res = _dot(l_ref[0], p_ref[...])
    for g8 in range(S5_LANE_GROUPS):
        o_ref[g8] = res[:, g8 * S5_K:(g8 + 1) * S5_K].astype(BF16)


def _s5_row_tile(n_rows):
    return next(t for t in S5_ROW_TILES if n_rows % t == 0)


def _s5_gather(su_wide, perm):
    slabs, r, wide = su_wide.shape
    rt = _s5_row_tile(r)
    return pl.pallas_call(
        _s5_gather_kernel,
        grid=(slabs, r // rt),
        in_specs=[pl.BlockSpec((1, rt, wide), lambda j, i: (j, i, 0)),
                  pl.BlockSpec((wide, wide), lambda j, i: (0, 0), pipeline_mode=pl.Buffered(1))],
        out_specs=pl.BlockSpec((S5_LANE_GROUPS, rt, S5_K), lambda j, i: (j, i, 0)),
        out_shape=jax.ShapeDtypeStruct((slabs * S5_LANE_GROUPS, r, S5_K), BF16),
        compiler_params=_cparams(("parallel", "parallel")),
        name="s5_gather",
    )(su_wide, perm)


def _s5_scatter_kernel(y_ref, p_ref, o_ref):
    y = jnp.concatenate([y_ref[g8] for g8 in range(S5_LANE_GROUPS)], axis=1)
    o_ref[0] = _nt_dot(y, p_ref[...]).astype(BF16)


def _s5_scatter(y_blocks, perm):
    g, r, k = y_blocks.shape
    slabs = g // S5_LANE_GROUPS
    rt = _s5_row_tile(r)
    return pl.pallas_call(
        _s5_scatter_kernel,
        grid=(slabs, r // rt),
        in_specs=[pl.BlockSpec((S5_LANE_GROUPS, rt, k), lambda j, i: (j, i, 0)),
                  pl.BlockSpec((S5_WIDE, S5_WIDE), lambda j, i: (0, 0), pipeline_mode=pl.Buffered(1))],
        out_specs=pl.BlockSpec((1, rt, S5_WIDE), lambda j, i: (j, i, 0)),
        out_shape=jax.ShapeDtypeStruct((slabs, r, S5_WIDE), BF16),
        compiler_params=_cparams(("parallel", "parallel")),
        name="s5_scatter",
    )(y_blocks, perm)


def _s5_kernel(x_ref, m1_ref, m2_ref, dec_ref, o_ref, p_ref, h_ref, *, n_batch, n_ctx_chunks, n_chunks):
    k, sw = S5_K, 2 * SSM_STATE
    p_ref[...] = _dot(x_ref[0], m1_ref[0])
    a1f, a2f = dec_ref[0, 0:1, :], dec_ref[0, 1:2, :]
    a1b, a2b = dec_ref[0, 2:3, :], dec_ref[0, 3:4, :]
    zero = jnp.zeros((n_batch, sw), F32)

    def step(j, carry):
        hf, hfs, hb, hbs = carry
        cb = jnp.where(j < n_ctx_chunks, n_ctx_chunks - 1 - j, n_chunks - 1 + n_ctx_chunks - j)
        rf = pl.multiple_of(j * n_batch, n_batch)
        rb = pl.multiple_of(cb * n_batch, n_batch)
        h_ref[pl.ds(rf, n_batch), 0:sw] = hf
        h_ref[pl.ds(rb, n_batch), sw:2 * sw] = hb
        sf = p_ref[pl.ds(rf, n_batch), k:k + sw]
        sfs = p_ref[pl.ds(rf, n_batch), k + sw:k + 2 * sw]
        sb = p_ref[pl.ds(rb, n_batch), k + 2 * sw:k + 3 * sw]
        sbs = p_ref[pl.ds(rb, n_batch), k + 3 * sw:k + 4 * sw]
        return (a1f * hf + a2f * hfs + sf, a1f * hfs - a2f * hf + sfs,
                a1b * hb + a2b * hbs + sb, a1b * hbs - a2b * hb + sbs)

    lax.fori_loop(0, n_chunks, step, (zero, zero, zero, zero))
    inter = _dot(h_ref[...].astype(BF16), m2_ref[0])
    o_ref[0] = (p_ref[:, 0:k] + inter).astype(BF16)


def _s5_mixer(x_blocks, m1, m2, dec, n_batch, n_ctx_chunks):
    g, r, k = x_blocks.shape
    n_chunks = r // n_batch
    kern = functools.partial(_s5_kernel, n_batch=n_batch, n_ctx_chunks=n_ctx_chunks, n_chunks=n_chunks)
    return pl.pallas_call(
        kern,
        grid=(g,),
        in_specs=[pl.BlockSpec((1, r, k), lambda i: (i, 0, 0)),
                  pl.BlockSpec((1, k, m1.shape[2]), lambda i: (i, 0, 0)),
                  pl.BlockSpec((1, m2.shape[1], k), lambda i: (i, 0, 0)),
                  pl.BlockSpec((1, 4, 2 * SSM_STATE), lambda i: (i, 0, 0))],
        out_specs=pl.BlockSpec((1, r, k), lambda i: (i, 0, 0)),
        out_shape=jax.ShapeDtypeStruct((g, r, k), BF16),
        scratch_shapes=[pltpu.VMEM((r, m1.shape[2]), F32), pltpu.VMEM((r, 4 * SSM_STATE), F32)],
        compiler_params=_cparams(("parallel",)),
        name="s5_mixer",
    )(x_blocks, m1, m2, dec)


def _qkconv_kernel(x_ref, w_ref, b_ref, s_ref, o_ref, *, lc):
    x = x_ref[0].astype(F32)
    lt = x.shape[0]
    row = lax.broadcasted_iota(jnp.int32, (lt, 1), 0)
    prev = jnp.where((row == 0) | (row == lc), 0.0, pltpu.roll(x, 1, axis=0))
    nxt = jnp.where((row == lc - 1) | (row == lt - 1), 0.0, pltpu.roll(x, lt - 1, axis=0))
    y = w_ref[0:1, :] * prev + w_ref[1:2, :] * x + w_ref[2:3, :] * nxt + b_ref[...]
    o_ref[0] = (y * jax.nn.sigmoid(y) * s_ref[...]).astype(BF16)


def _qkconv(proj, conv_w, conv_b, post_scale, col0, lc):
    b, lt, _ = proj.shape
    width = conv_w.shape[1]
    cw = 256
    blk0 = col0 // cw
    kern = functools.partial(_qkconv_kernel, lc=lc)
    return pl.pallas_call(
        kern,
        grid=(b, width // cw),
        in_specs=[pl.BlockSpec((1, lt, cw), lambda bi, j: (bi, 0, blk0 + j)),
                  pl.BlockSpec((QK_CONV, cw), lambda bi, j: (0, j)),
                  pl.BlockSpec((1, cw), lambda bi, j: (0, j)),
                  pl.BlockSpec((1, cw), lambda bi, j: (0, j))],
        out_specs=pl.BlockSpec((1, lt, cw), lambda bi, j: (bi, 0, j)),
        out_shape=jax.ShapeDtypeStruct((b, lt, width), BF16),
        compiler_params=_cparams(("parallel", "parallel")),
        name="mlstm_qk_conv",
    )(proj, conv_w, conv_b.reshape(1, width), post_scale)


def _log_sigmoid(x):
    return jnp.minimum(x, 0.0) - jnp.log1p(jnp.exp(-jnp.abs(x)))


def _mlstm_masks():
    t, nv = MLSTM_CHUNK, 2 * N_MLSTM_HEADS
    r_i = jnp.arange(t)[:, None]
    c_i = jnp.arange(t)[None, :]
    cum = jnp.stack([r_i <= c_i, r_i >= c_i]).astype(BF16)
    neg = jnp.where(jnp.stack([c_i <= r_i, c_i >= r_i]), 0.0, -jnp.inf).astype(F32)
    sel = (jnp.arange(nv * LANES)[None, :] // LANES == jnp.arange(nv)[:, None]).astype(BF16)
    return cum, neg, sel


def _mlstm_chunk(q, k, v, ig, f_row, f_col, i_col, neg, cn0, m0, rev):
    t, dh = MLSTM_CHUNK, MLSTM_HEAD
    ones = jnp.ones((t, dh), BF16)
    r_row = f_row - ig
    f_end = f_row[:, 0:1] if rev else f_row[:, t - 1:t]
    log_d = f_col - r_row + neg
    log_inter = f_col + m0
    m_t = jnp.maximum(jnp.max(log_d, axis=1, keepdims=True), log_inter)
    s_qk = (_nt_dot(q, k) * jnp.exp(log_d - m_t)).astype(BF16)
    e_inter = jnp.exp(log_inter - m_t)
    intra = _dot(s_qk, jnp.concatenate([v, ones], axis=1))
    inter = _dot(q, cn0.astype(BF16))
    num = intra[:, 0:dh] + e_inter * inter[:, 0:dh]
    den = intra[:, dh:2 * dh] + e_inter * inter[:, dh:2 * dh]
    h = num / jnp.maximum(jnp.abs(den), jnp.exp(-m_t))
    m_loc = f_end - jnp.min(r_row, axis=1, keepdims=True)
    e_end = jnp.exp(f_end - (f_col - i_col) - m_loc)
    w = jnp.concatenate([e_end * v.astype(F32), e_end], axis=1).astype(BF16)
    d_cn = _tn_dot(k, w)
    m_new = jnp.maximum(f_end + m0, m_loc)
    a = jnp.exp(f_end + m0 - m_new)
    bb = jnp.exp(m_loc - m_new)
    return h, a * cn0 + bb * d_cn, m_new


def _mlstm_kernel(q_ref, k_ref, v_ref, o_ref, gt_ref, gb_ref, g_ref, cum_ref, neg_ref, sel_ref,
                  out_ref, acc_ref, cn_ref, m_ref, *, n_ctx_chunks, n_chunks):
    t, dh, nh = MLSTM_CHUNK, MLSTM_HEAD, N_MLSTM_HEADS
    acc_ref[...] = jnp.zeros_like(acc_ref)
    cn_ref[...] = jnp.zeros_like(cn_ref)
    m_ref[...] = jnp.zeros_like(m_ref)

    def step(j, carry):
        for d in range(2):
            rev = d == 1
            if rev:
                c = jnp.where(j < n_ctx_chunks, n_ctx_chunks - 1 - j, n_chunks - 1 + n_ctx_chunks - j)
            else:
                c = j
            r0 = pl.multiple_of(c * t, t)
            gates = gt_ref[0, :, pl.ds(r0, t)] + gb_ref[...]
            ig = gates[2 * nh * d:2 * nh * d + nh]
            lf = _log_sigmoid(gates[2 * nh * d + nh:2 * nh * (d + 1)])
            f_rows = _dot_split(lf, cum_ref[d])
            rows = jnp.concatenate([f_rows, ig], axis=0)
            hi = rows.astype(BF16)
            lo = (rows - hi.astype(F32)).astype(BF16)
            cols = _tn_dot(hi, sel_ref[...]) + _tn_dot(lo, sel_ref[...])
            for hd in range(nh):
                idx = d * nh + hd
                cs = slice(hd * dh, (hd + 1) * dh)
                h, cn, m = _mlstm_chunk(q_ref[0, pl.ds(r0, t), cs], k_ref[0, pl.ds(r0, t), cs],
                                        v_ref[0, pl.ds(r0, t), cs], ig[hd:hd + 1], f_rows[hd:hd + 1],
                                        cols[:, hd * dh:(hd + 1) * dh], cols[:, (nh + hd) * dh:(nh + hd + 1) * dh],
                                        neg_ref[d], cn_ref[idx], m_ref[idx:idx + 1, 0:1], rev)
                acc_ref[pl.ds(r0, t), cs] += h
                cn_ref[idx] = cn
                m_ref[idx:idx + 1, :] = jnp.broadcast_to(m, (1, LANES))
        return carry

    lax.fori_loop(0, n_chunks, step, 0)

    def finish(c, carry):
        r0 = pl.multiple_of(c * t, t)
        for hd in range(nh):
            cs = slice(hd * dh, (hd + 1) * dh)
            hsum = acc_ref[pl.ds(r0, t), cs]
            y = hsum * lax.rsqrt(jnp.mean(hsum * hsum, axis=-1, keepdims=True) + EPS) * g_ref[:, cs]
            out_ref[0, pl.ds(r0, t), cs] = (y * jax.nn.sigmoid(o_ref[0, pl.ds(r0, t), cs].astype(F32))).astype(BF16)
        return carry

    lax.fori_loop(0, n_chunks, finish, 0)


def _mlstm(qk, proj, gates_t, gate_b, norm_g, col_v, col_o, lc):
    b, lt, _ = qk.shape
    w, t = D_MLSTM, MLSTM_CHUNK
    kern = functools.partial(_mlstm_kernel, n_ctx_chunks=lc // t, n_chunks=lt // t)
    cum, neg, sel = _mlstm_masks()
    return pl.pallas_call(
        kern,
        grid=(b,),
        in_specs=[pl.BlockSpec((1, lt, w), lambda bi: (bi, 0, 0)),
                  pl.BlockSpec((1, lt, w), lambda bi: (bi, 0, 1)),
                  pl.BlockSpec((1, lt, w), lambda bi: (bi, 0, col_v // w)),
                  pl.BlockSpec((1, lt, w), lambda bi: (bi, 0, col_o // w)),
                  pl.BlockSpec((1, N_GATES, lt), lambda bi: (bi, 0, 0)),
                  pl.BlockSpec((N_GATES, 1), lambda bi: (0, 0)),
                  pl.BlockSpec((1, w), lambda bi: (0, 0)),
                  pl.BlockSpec(cum.shape, lambda bi: (0, 0, 0)),
                  pl.BlockSpec(neg.shape, lambda bi: (0, 0, 0)),
                  pl.BlockSpec(sel.shape, lambda bi: (0, 0))],
        out_specs=pl.BlockSpec((1, lt, w), lambda bi: (bi, 0, 0)),
        out_shape=jax.ShapeDtypeStruct((b, lt, w), BF16),
        scratch_shapes=[pltpu.VMEM((lt, w), F32),
                        pltpu.VMEM((2 * N_MLSTM_HEADS, MLSTM_HEAD, 2 * MLSTM_HEAD), F32),
                        pltpu.VMEM((2 * N_MLSTM_HEADS, LANES), F32)],
        compiler_params=_cparams(("parallel",)),
        name="mlstm_mixer",
    )(qk, qk, proj, proj, gates_t, gate_b, norm_g.reshape(1, w), cum, neg, sel)


def _attn_kernel(sc_ref, q_ref, k_ref, v_ref, lam_ref, g_ref, o_ref, vext_ref, *, lc, ctx_tiles):
    i = pl.program_id(2)
    dv = DIFF_V_HEAD
    lp = lam_ref[...]
    lam = (jnp.exp(jnp.sum(lp[0:1] * lp[1:2], axis=1, keepdims=True))
           - jnp.exp(jnp.sum(lp[2:3] * lp[3:4], axis=1, keepdims=True)) + sc_ref[0])
    lane = lax.broadcasted_iota(jnp.int32, (1, dv), 1)

    n_heads = q_ref.shape[2] // dv

    @pl.when(i == 0)
    def _():
        for h in range(n_heads):
            vext_ref[:, 2 * h * dv:(2 * h + 1) * dv] = v_ref[0, :, h * dv:(h + 1) * dv]
            vext_ref[:, (2 * h + 1) * dv:(2 * h + 2) * dv] = jnp.ones((v_ref.shape[1], dv), BF16)

    def attend(nk, kc):
        for h in range(n_heads):
            hs = slice(h * dv, (h + 1) * dv)
            q = q_ref[0, :, hs]
            outs = []
            for c in range(2):
                keep = (lane < DIFF_QK_HEAD) if c == 0 else (lane >= DIFF_QK_HEAD)
                qc = jnp.where(keep, q, jnp.zeros_like(q))
                m_run, acc = None, None
                for k0 in range(0, nk, kc):
                    s = _nt_dot(qc, k_ref[0, k0:k0 + kc, hs])
                    m_j = jnp.max(s, axis=1, keepdims=True)
                    m_new = m_j if m_run is None else jnp.maximum(m_run, m_j)
                    pv = _dot(jnp.exp(s - m_new).astype(BF16), vext_ref[k0:k0 + kc, 2 * h * dv:(2 * h + 2) * dv])
                    acc = pv if m_run is None else acc * jnp.exp(m_run - m_new) + pv
                    m_run = m_new
                outs.append(acc[:, 0:dv] / acc[:, dv:2 * dv])
            o = outs[0] - lam * outs[1]
            y = o * lax.rsqrt(jnp.mean(o * o, axis=-1, keepdims=True) + EPS) * g_ref[...]
            o_ref[0, :, hs] = (y * sc_ref[1]).astype(BF16)

    @pl.when(i < ctx_tiles)
    def _():
        attend(lc, lc)

    @pl.when(i >= ctx_tiles)
    def _():
        nk = k_ref.shape[1]
        attend(nk, ATTN_KEY_CHUNK if nk % ATTN_KEY_CHUNK == 0 else nk)


def _attention(scal, proj, lam_p, subln_g, col_q, col_k, col_v, lc):
    b, lt, _ = proj.shape
    w, dv = D_DIFF, DIFF_V_HEAD
    kern = functools.partial(_attn_kernel, lc=lc, ctx_tiles=lc // ROW_TILE)
    return pl.pallas_call(
        kern,
        grid=(b, 1, lt // ROW_TILE),
        in_specs=[pl.BlockSpec(memory_space=pltpu.SMEM),
                  pl.BlockSpec((1, ROW_TILE, w), lambda bi, hi, i: (bi, i, col_q // w)),
                  pl.BlockSpec((1, lt, w), lambda bi, hi, i: (bi, 0, col_k // w)),
                  pl.BlockSpec((1, lt, w), lambda bi, hi, i: (bi, 0, col_v // w)),
                  pl.BlockSpec((4, DIFF_QK_HEAD), lambda bi, hi, i: (0, 0)),
                  pl.BlockSpec((1, dv), lambda bi, hi, i: (0, 0))],
        out_specs=pl.BlockSpec((1, ROW_TILE, w), lambda bi, hi, i: (bi, i, 0)),
        out_shape=jax.ShapeDtypeStruct((b, lt, w), BF16),
        scratch_shapes=[pltpu.VMEM((lt, 2 * w), BF16)],
        compiler_params=_cparams(("parallel", "parallel", "arbitrary")),
        name="diff_attention",
    )(scal, proj, proj, proj, lam_p, subln_g.reshape(1, dv))


def _silu(x):
    return x * jax.nn.sigmoid(x)


def _merge_kernel(x_ref, mod_ref, y5_ref, sz_ref, hb_ref, mz_ref, oc_ref, dz_ref, gl_ref,
                  gw_ref, gb_ref, wa_ref, wb_ref, wc_ref, wo_ref, o_ref, *, d_model):
    f = lambda r: r[0].astype(F32)
    rows = x_ref.shape[1]
    y = jnp.concatenate([y5_ref[j, :, 0].reshape(rows, LANES) for j in range(S5_SLABS)], axis=1).astype(F32)
    ge = 0.5 * y * (1.0 + jnp.tanh(math.sqrt(2.0 / math.pi) * (y + 0.044715 * (y * y * y))))
    t = _dot(ge.astype(BF16), gw_ref[...]) + gb_ref[...]
    ya = t[:, 0:D_SSM] * jax.nn.sigmoid(t[:, D_SSM:2 * D_SSM])
    pa = _dot((ya * _silu(f(sz_ref))).astype(BF16), wa_ref[...])
    pb = _dot((f(hb_ref) * _silu(f(mz_ref))).astype(BF16), wb_ref[...])
    pc = _dot((f(oc_ref) * _silu(f(dz_ref))).astype(BF16), wc_ref[...])
    gates = jax.nn.sigmoid(f(gl_ref))
    m = (gates[:, 0:d_model] * pa + gates[:, d_model:2 * d_model] * pb
         + gates[:, 2 * d_model:3 * d_model] * pc)
    out = _dot(m.astype(BF16), wo_ref[...])
    o_ref[0] = x_ref[0] + mod_ref[0, :, 2 * d_model:3 * d_model] * out


def _merge(x_all, mods3, y5, hb, oc, proj, glu_w, glu_b, wa, wb, wc, wo, cols, n_batch, lc, skip_ctx):
    b, lt, d = x_all.shape
    ctx_tiles = lc // ROW_TILE
    t0 = ctx_tiles if skip_ctx else 0
    w = D_SSM
    blk = lambda name: cols[name] // w
    blk_rows = ROW_TILE // S5_CHUNK
    tok = lambda j: pl.BlockSpec((1, ROW_TILE, w), lambda bi, i: (bi, i + t0, j))
    full = lambda a: pl.BlockSpec(a.shape, lambda bi, i: (0,) * a.ndim)
    kern = functools.partial(_merge_kernel, d_model=d)
    return pl.pallas_call(
        kern,
        grid=(b, lt // ROW_TILE - t0),
        in_specs=[pl.BlockSpec((1, ROW_TILE, d), lambda bi, i: (bi, i + t0, 0)),
                  pl.BlockSpec((1, 1, 3 * d), lambda bi, i: (jnp.where(i + t0 < ctx_tiles, n_batch, bi), 0, 0)),
                  pl.BlockSpec((S5_SLABS, blk_rows, 1, S5_CHUNK, LANES), lambda bi, i: (0, i + t0, bi, 0, 0)),
                  tok(blk("sz")), tok(0), tok(blk("mz")), tok(0), tok(blk("dz")),
                  pl.BlockSpec((1, ROW_TILE, N_BRANCH * d), lambda bi, i: (bi, i + t0, 0)),
                  full(glu_w), full(glu_b), full(wa), full(wb), full(wc), full(wo)],
        out_specs=pl.BlockSpec((1, ROW_TILE, d), lambda bi, i: (bi, i, 0)),
        out_shape=jax.ShapeDtypeStruct((b, lt - t0 * ROW_TILE, d), F32),
        compiler_params=_cparams(("parallel", "parallel")),
        name="merge_outproj",
    )(x_all, mods3, y5, proj, hb, proj, oc, proj, proj, glu_w, glu_b, wa, wb, wc, wo)


def _rope_tables(lc, l):
    half = DIFF_QK_HEAD // 2
    rows = l // GRID_W
    row = jnp.repeat(jnp.arange(rows), GRID_W).astype(F32)
    col = jnp.tile(jnp.arange(GRID_W), rows).astype(F32)
    inv = jnp.power(ROPE_BASE, -jnp.arange(0, half, 2, dtype=F32) / half)
    ang_r, ang_c = row[:, None] * inv, col[:, None] * inv
    cos64 = jnp.concatenate([jnp.cos(ang_r), jnp.cos(ang_r), jnp.cos(ang_c), jnp.cos(ang_c)], axis=1)
    sin64 = jnp.concatenate([-jnp.sin(ang_r), jnp.sin(ang_r), -jnp.sin(ang_c), jnp.sin(ang_c)], axis=1)
    cos_t = jnp.concatenate([jnp.ones((lc, DIFF_QK_HEAD), F32), cos64], axis=0)
    sin_t = jnp.concatenate([jnp.zeros((lc, DIFF_QK_HEAD), F32), sin64], axis=0)
    return jnp.tile(cos_t, (1, 2)), jnp.tile(sin_t, (1, 2))


def kernel(x, c, ctx, c_ctx, norm_g, ada_w, ada_b, w_in, ssm_lam_re, ssm_lam_im, ssm_log_step, ssm_b_re, ssm_b_im, ssm_c_re, ssm_c_im, ssm_d, ssm_glu_w, ssm_glu_b, w_ssm_out, ml_conv_w, ml_conv_b, ml_gate_b, ml_norm_g, w_ml_out, da_qnorm_g, da_knorm_g, da_lambda, da_subln_g, w_da_out, w_out):
    n_batch, l, d = x.shape
    lc = ctx.shape[1]
    lt = lc + l
    depth = w_in.shape[0]
    cols = _cols(d)
    t5 = S5_CHUNK
    n_rows = (lt // t5) * n_batch
    assert lc % ROW_TILE == 0 and l % ROW_TILE == 0 and lc % MLSTM_CHUNK == 0 and n_batch % 8 == 0
    assert MLSTM_CHUNK == MLSTM_HEAD == LANES

    x_all = jnp.concatenate([ctx, x], axis=1)

    mod_rows = -(-(n_batch + 1) // 8) * 8
    cvec = jnp.zeros((mod_rows, d), F32).at[:n_batch].set(c).at[n_batch].set(c_ctx)
    mods = _modulation(cvec, ada_w, ada_b)

    g0 = 2 * D_SSM + 5 * D_MLSTM
    g1 = g0 + N_GATES
    m0 = g1 + 4 * D_DIFF
    w_p = jnp.concatenate([w_in[:, :, m0:], w_in[:, :, D_SSM:g0], w_in[:, :, g1:m0]], axis=2).astype(BF16)
    w_su = w_in[:, :, :D_SSM].astype(BF16)
    w_gt = jnp.swapaxes(w_in[:, :, g0:g1], 1, 2).astype(BF16)

    cos_t, sin_t = _rope_tables(lc, l)
    ones_bd = jnp.kron(jnp.eye(2 * LANES // DIFF_QK_HEAD, dtype=F32),
                       jnp.ones((DIFF_QK_HEAD, DIFF_QK_HEAD), F32)).astype(BF16)
    qk_scale = jnp.concatenate([jnp.ones((1, D_MLSTM), F32), jnp.full((1, D_MLSTM), MLSTM_HEAD ** -0.5, F32)], 1)
    perm = _s5_lane_permutation()
    s5_m1, s5_m2, s5_dec = jax.vmap(_s5_operators)(ssm_lam_re, ssm_lam_im, ssm_log_step, ssm_b_re, ssm_b_im,
                                                   ssm_c_re, ssm_c_im, ssm_d)

    for li in range(depth):
        last = li == depth - 1
        mods3 = mods[li].reshape(mod_rows, 1, 3 * d)
        qg = jnp.tile(da_qnorm_g[li].astype(F32), D_DIFF // DIFF_QK_HEAD).reshape(1, D_DIFF)
        kg = jnp.tile(da_knorm_g[li].astype(F32), D_DIFF // DIFF_QK_HEAD).reshape(1, D_DIFF)
        proj, su, gates_t = _inproj(x_all, mods3, norm_g[li], w_p[li], w_su[li], w_gt[li], cos_t, sin_t, qg, kg,
                                    ones_bd, cols, n_batch, lc)

        xb = _s5_gather(su.reshape(S5_SLABS, n_rows, S5_WIDE), perm)
        yb = _s5_mixer(xb, s5_m1[li], s5_m2[li], s5_dec[li], n_batch, lc // t5)
        y5 = _s5_scatter(yb, perm).reshape(su.shape)

        qk = _qkconv(proj, ml_conv_w[li], ml_conv_b[li], qk_scale, cols["mqk"], lc)
        hb = _mlstm(qk, proj, gates_t, ml_gate_b[li].astype(F32).reshape(N_GATES, 1), ml_norm_g[li],
                    cols["mv"], cols["mo"], lc)

        lam_init = 0.8 - 0.6 * math.exp(-0.3 * li)
        scal = jnp.array([lam_init, 1.0 - lam_init], F32)
        oc = _attention(scal, proj, da_lambda[li].astype(F32), da_subln_g[li].astype(F32),
                        cols["dq"], cols["dk"], cols["dv"], lc)

        x_all = _merge(x_all, mods3, y5, hb, oc, proj, ssm_glu_w[li].astype(BF16),
                       ssm_glu_b[li].astype(F32).reshape(1, -1), w_ssm_out[li].astype(BF16),
                       w_ml_out[li].astype(BF16), w_da_out[li].astype(BF16), w_out[li].astype(BF16),
                       cols, n_batch, lc, skip_ctx=last)
    return x_all
```

```python
import functools
import math

import jax
import jax.numpy as jnp
from jax import lax
from jax.experimental import pallas as pl
from jax.experimental.pallas import tpu as pltpu

F32 = jnp.float32
BF16 = jnp.bfloat16
HIGHEST = lax.Precision.HIGHEST

EPS = 1e-6
GRID_W = 64
ROPE_BASE = 10000.0
LANES = 128

D_SSM = 512
SSM_GROUP = 16
N_SSM_GROUPS = D_SSM // SSM_GROUP
SSM_STATE = 64
S5_CHUNK = 16
S5_K = S5_CHUNK * SSM_GROUP
S5_LANE_GROUPS = LANES // SSM_GROUP
S5_SLABS = D_SSM // LANES
S5_WIDE = S5_CHUNK * LANES
S5_ROW_TILES = (768, 512, 256)
D_MLSTM = 512
N_MLSTM_HEADS = 4
MLSTM_HEAD = D_MLSTM // N_MLSTM_HEADS
MLSTM_CHUNK = 128
QK_CONV = 3
N_GATES = 4 * N_MLSTM_HEADS
N_DIFF_HEADS = 4
DIFF_QK_HEAD = 64
DIFF_V_HEAD = 2 * DIFF_QK_HEAD
D_DIFF = N_DIFF_HEADS * DIFF_V_HEAD
ATTN_KEY_CHUNK = 768
N_BRANCH = 3

ROW_TILE = 256
VMEM_LIMIT = 56 * 1024 * 1024


def _cols(d_model):
    off = {"gl": 0}
    o = N_BRANCH * d_model
    for name, w in (("sz", D_SSM), ("mqk", 2 * D_MLSTM), ("mv", D_MLSTM), ("mo", D_MLSTM),
                    ("mz", D_MLSTM), ("dq", D_DIFF), ("dk", D_DIFF), ("dv", D_DIFF), ("dz", D_DIFF)):
        off[name] = o
        o += w
    off["total"] = o
    return off


def _cparams(sem):
    return pltpu.CompilerParams(dimension_semantics=sem, vmem_limit_bytes=VMEM_LIMIT)


def _nt_dot(a, b):
    return lax.dot_general(a, b, (((1,), (1,)), ((), ())), preferred_element_type=F32)


def _tn_dot(a, b):
    return lax.dot_general(a, b, (((0,), (0,)), ((), ())), preferred_element_type=F32)


def _dot(a, b):
    return jnp.dot(a, b, preferred_element_type=F32)


def _dot_split(x, w):
    hi = x.astype(BF16)
    lo = (x - hi.astype(F32)).astype(BF16)
    return _dot(hi, w) + _dot(lo, w)


def _mods_kernel(c_ref, w_ref, b_ref, o_ref):
    c = c_ref[...]
    act = c * jax.nn.sigmoid(c)
    o_ref[0] = jnp.dot(act, w_ref[0], preferred_element_type=F32, precision=HIGHEST) + b_ref[0]


def _modulation(cvec, ada_w, ada_b):
    depth, d, d3 = ada_w.shape
    rows = cvec.shape[0]
    return pl.pallas_call(
        _mods_kernel,
        grid=(depth, d3 // d),
        in_specs=[pl.BlockSpec((rows, d), lambda l, j: (0, 0)),
                  pl.BlockSpec((1, d, d), lambda l, j: (l, 0, j)),
                  pl.BlockSpec((1, 1, d), lambda l, j: (l, 0, j))],
        out_specs=pl.BlockSpec((1, rows, d), lambda l, j: (l, 0, j)),
        out_shape=jax.ShapeDtypeStruct((depth, rows, d3), F32),
        compiler_params=_cparams(("arbitrary", "arbitrary")),
        name="adaln_modulation",
    )(cvec, ada_w, ada_b.reshape(depth, 1, d3))


def _qk_norm_rope(x, gain, cos, sin, ones_bd, scale):
    half = ones_bd.shape[0]
    ms = jnp.concatenate([_dot_split(x[:, c0:c0 + half] * x[:, c0:c0 + half], ones_bd)
                          for c0 in range(0, x.shape[1], half)], axis=1) * (1.0 / DIFF_QK_HEAD)
    y = x * lax.rsqrt(ms + EPS) * gain
    lane = lax.broadcasted_iota(jnp.int32, (1, LANES), 1)
    first_half = (lane // 16) % 2 == 0
    outs = []
    for c0 in range(0, y.shape[1], LANES):
        yb = y[:, c0:c0 + LANES]
        partner = jnp.where(first_half, pltpu.roll(yb, LANES - 16, axis=1), pltpu.roll(yb, 16, axis=1))
        outs.append(yb * cos + partner * sin)
    return jnp.concatenate(outs, axis=1) * scale


def _inproj_kernel(x_ref, mod_ref, g_ref, w_ref, wsu_ref, wgt_ref, cos_ref, sin_ref, qg_ref, kg_ref, ones_ref,
                   o_ref, su_ref, ogt_ref, *, d_model, col_chunk, col_q, col_k, sigmoid_cols, silu_cols):
    x = x_ref[0]
    ms = jnp.mean(x * x, axis=-1, keepdims=True)
    y = x * lax.rsqrt(ms + EPS) * g_ref[...]
    sh = mod_ref[0, :, 0:d_model]
    sc = mod_ref[0, :, d_model:2 * d_model]
    hb = (y * (1.0 + sc) + sh).astype(BF16)
    n_out = o_ref.shape[-1]
    for c0 in range(0, n_out, col_chunk):
        acc = _dot(hb, w_ref[:, c0:c0 + col_chunk])
        if c0 == col_q:
            acc = _qk_norm_rope(acc, qg_ref[...], cos_ref[...], sin_ref[...], ones_ref[...], DIFF_QK_HEAD ** -0.5)
        elif c0 == col_k:
            acc = _qk_norm_rope(acc, kg_ref[...], cos_ref[...], sin_ref[...], ones_ref[...], 1.0)
        elif c0 in sigmoid_cols:
            acc = jax.nn.sigmoid(acc)
        elif c0 in silu_cols:
            acc = acc * jax.nn.sigmoid(acc)
        o_ref[0, :, c0:c0 + col_chunk] = acc.astype(BF16)
    rows = x.shape[0]
    for j in range(S5_SLABS):
        su = _dot(hb, wsu_ref[:, j * LANES:(j + 1) * LANES]).astype(BF16)
        su_ref[j, :, 0] = su.reshape(rows // S5_CHUNK, S5_CHUNK, LANES)
    ogt_ref[0] = _nt_dot(wgt_ref[...], hb)


def _inproj(x_all, mods3, norm_g, w_p, w_su, w_gt, cos_t, sin_t, qg, kg, ones_bd, cols, n_batch, lc):
    b, lt, d = x_all.shape
    n_out = w_p.shape[1]
    ctx_tiles = lc // ROW_TILE
    chunk = D_DIFF
    assert all(cols[k] % chunk == 0 for k in ("dq", "dk", "sz", "mo", "mz", "dz")) and cols["sz"] == N_BRANCH * d
    sigmoid_cols = tuple(range(cols["gl"], cols["sz"], chunk)) + (cols["mo"],)
    silu_cols = (cols["sz"], cols["mz"], cols["dz"])
    kern = functools.partial(_inproj_kernel, d_model=d, col_chunk=chunk, col_q=cols["dq"], col_k=cols["dk"],
                             sigmoid_cols=sigmoid_cols, silu_cols=silu_cols)
    blk_rows = ROW_TILE // S5_CHUNK
    return pl.pallas_call(
        kern,
        grid=(b, lt // ROW_TILE),
        in_specs=[pl.BlockSpec((1, ROW_TILE, d), lambda bi, i: (bi, i, 0)),
                  pl.BlockSpec((1, 1, 3 * d), lambda bi, i: (jnp.where(i < ctx_tiles, n_batch, bi), 0, 0)),
                  pl.BlockSpec((1, d), lambda bi, i: (0, 0)),
                  pl.BlockSpec((d, n_out), lambda bi, i: (0, 0), pipeline_mode=pl.Buffered(1)),
                  pl.BlockSpec((d, D_SSM), lambda bi, i: (0, 0)),
                  pl.BlockSpec((N_GATES, d), lambda bi, i: (0, 0)),
                  pl.BlockSpec((ROW_TILE, LANES), lambda bi, i: (i, 0)),
                  pl.BlockSpec((ROW_TILE, LANES), lambda bi, i: (i, 0)),
                  pl.BlockSpec((1, D_DIFF), lambda bi, i: (0, 0)),
                  pl.BlockSpec((1, D_DIFF), lambda bi, i: (0, 0)),
                  pl.BlockSpec(ones_bd.shape, lambda bi, i: (0, 0))],
        out_specs=[pl.BlockSpec((1, ROW_TILE, n_out), lambda bi, i: (bi, i, 0)),
                   pl.BlockSpec((S5_SLABS, blk_rows, 1, S5_CHUNK, LANES), lambda bi, i: (0, i, bi, 0, 0)),
                   pl.BlockSpec((1, N_GATES, ROW_TILE), lambda bi, i: (bi, 0, i))],
        out_shape=[jax.ShapeDtypeStruct((b, lt, n_out), BF16),
                   jax.ShapeDtypeStruct((S5_SLABS, lt // S5_CHUNK, b, S5_CHUNK, LANES), BF16),
                   jax.ShapeDtypeStruct((b, N_GATES, lt), F32)],
        compiler_params=_cparams(("parallel", "parallel")),
        name="adaln_inproj",
    )(x_all, mods3, norm_g.reshape(1, d), w_p, w_su, w_gt, cos_t, sin_t, qg, kg, ones_bd)


def _s5_operators(lam_re, lam_im, log_step, b_re, b_im, c_re, c_im, d_skip):
    t = S5_CHUNK
    g, p, n = N_SSM_GROUPS, SSM_STATE, SSM_GROUP
    hp = HIGHEST
    tin = jnp.zeros((g, t, n, t, n), F32)
    m1_states, m2_rows, decs = [], [], []
    s_idx = jnp.arange(t)
    for d in range(2):
        lre = jnp.minimum(lam_re[d].astype(F32), -1e-4)
        lim = lam_im[d].astype(F32)
        dt = jnp.exp(log_step[d].astype(F32))[:, None]
        ld_re, ld_im = lre * dt, lim * dt
        mag = jnp.exp(ld_re)
        ab_re, ab_im = mag * jnp.cos(ld_im), mag * jnp.sin(ld_im)
        den = lre * lre + lim * lim
        num_re, num_im = ab_re - 1.0, ab_im
        coef_re = (num_re * lre + num_im * lim) / den
        coef_im = (num_im * lre - num_re * lim) / den
        bre, bim = b_re[d].astype(F32), b_im[d].astype(F32)
        bb_re = coef_re[..., None] * bre - coef_im[..., None] * bim
        bb_im = coef_re[..., None] * bim + coef_im[..., None] * bre
        cre, cim = c_re[d].astype(F32), c_im[d].astype(F32)
        k_pow = jnp.arange(t + 1, dtype=F32)[:, None, None]
        pmag = jnp.exp(k_pow * ld_re)
        pw_re, pw_im = pmag * jnp.cos(k_pow * ld_im), pmag * jnp.sin(k_pow * ld_im)
        ab_b_re = pw_re[..., None] * bb_re - pw_im[..., None] * bb_im
        ab_b_im = pw_re[..., None] * bb_im + pw_im[..., None] * bb_re
        kern = (jnp.einsum("gop,kgpn->kgon", cre, ab_b_re, precision=hp)
                - jnp.einsum("gop,kgpn->kgon", cim, ab_b_im, precision=hp))
        lag = (s_idx[None, :] - s_idx[:, None]) if d == 0 else (s_idx[:, None] - s_idx[None, :])
        valid = lag >= 0
        kk = kern[jnp.clip(lag, 0, t)]
        kk = jnp.where(valid[:, :, None, None, None], kk, 0.0)
        tin = tin + jnp.transpose(kk, (2, 0, 4, 1, 3))
        pw_idx = (t - 1 - s_idx) if d == 0 else s_idx
        st_re = jnp.transpose(ab_b_re[pw_idx], (1, 0, 3, 2))
        st_im = jnp.transpose(ab_b_im[pw_idx], (1, 0, 3, 2))
        m1_states.append(jnp.concatenate([st_re, st_im], -1).reshape(g, t * n, 2 * p))
        m1_states.append(jnp.concatenate([st_im, st_re], -1).reshape(g, t * n, 2 * p))
        rp = (s_idx + 1) if d == 0 else (t - s_idx)
        ca_re = cre[None] * pw_re[rp][:, :, None, :] - cim[None] * pw_im[rp][:, :, None, :]
        ca_im = cre[None] * pw_im[rp][:, :, None, :] + cim[None] * pw_re[rp][:, :, None, :]
        rd = jnp.concatenate([jnp.transpose(ca_re, (1, 3, 0, 2)), -jnp.transpose(ca_im, (1, 3, 0, 2))], 1)
        m2_rows.append(rd.reshape(g, 2 * p, t * n))
        decs.append(jnp.concatenate([pw_re[t], pw_re[t]], -1))
        decs.append(jnp.concatenate([-pw_im[t], pw_im[t]], -1))
    eye = jnp.eye(t * n, dtype=F32).reshape(1, t, n, t, n)
    tin = tin + eye * d_skip.astype(F32).reshape(g, 1, n, 1, 1)
    m1 = jnp.concatenate([tin.reshape(g, t * n, t * n)] + m1_states, axis=-1)
    m2 = jnp.concatenate(m2_rows, axis=1)
    dec = jnp.stack(decs, axis=1)
    return m1.astype(BF16), m2.astype(BF16), dec


def _s5_lane_permutation():
    src = jnp.arange(S5_WIDE)
    s, rem = src // LANES, src % LANES
    g8, n = rem // SSM_GROUP, rem % SSM_GROUP
    dst = g8 * S5_K + s * SSM_GROUP + n
    return (dst[:, None] == jnp.arange(S5_WIDE)[None, :]).astype(BF16)


def _s5_gather_kernel(l_ref, p_ref, o_ref):
    res = _dot(l_ref[0], p_ref[...])
    for g8 in range(S5_LANE_GROUPS):
        o_ref[g8] = res[:, g8 * S5_K:(g8 + 1) * S5_K].astype(BF16)


def _s5_row_tile(n_rows):
    return next(t for t in S5_ROW_TILES if n_rows % t == 0)


def _s5_gather(su_wide, perm):
    slabs, r, wide = su_wide.shape
    rt = _s5_row_tile(r)
    return pl.pallas_call(
        _s5_gather_kernel,
        grid=(slabs, r // rt),
        in_specs=[pl.BlockSpec((1, rt, wide), lambda j, i: (j, i, 0)),
                  pl.BlockSpec((wide, wide), lambda j, i: (0, 0), pipeline_mode=pl.Buffered(1))],
        out_specs=pl.BlockSpec((S5_LANE_GROUPS, rt, S5_K), lambda j, i: (j, i, 0)),
        out_shape=jax.ShapeDtypeStruct((slabs * S5_LANE_GROUPS, r, S5_K), BF16),
        compiler_params=_cparams(("parallel", "parallel")),
        name="s5_gather",
    )(su_wide, perm)


def _s5_scatter_kernel(y_ref, p_ref, o_ref):
    y = jnp.concatenate([y_ref[g8] for g8 in range(S5_LANE_GROUPS)], axis=1)
    o_ref[0] = _nt_dot(y, p_ref[...]).astype(BF16)


def _s5_scatter(y_blocks, perm):
    g, r, k = y_blocks.shape
    slabs = g // S5_LANE_GROUPS
    rt = _s5_row_tile(r)
    return pl.pallas_call(
        _s5_scatter_kernel,
        grid=(slabs, r // rt),
        in_specs=[pl.BlockSpec((S5_LANE_GROUPS, rt, k), lambda j, i: (j, i, 0)),
                  pl.BlockSpec((S5_WIDE, S5_WIDE), lambda j, i: (0, 0), pipeline_mode=pl.Buffered(1))],
        out_specs=pl.BlockSpec((1, rt, S5_WIDE), lambda j, i: (j, i, 0)),
        out_shape=jax.ShapeDtypeStruct((slabs, r, S5_WIDE), BF16),
        compiler_params=_cparams(("parallel", "parallel")),
        name="s5_scatter",
    )(y_blocks, perm)


def _s5_kernel(x_ref, m1_ref, m2_ref, dec_ref, o_ref, p_ref, h_ref, *, n_batch, n_ctx_chunks, n_chunks):
    k, sw = S5_K, 2 * SSM_STATE
    p_ref[...] = _dot(x_ref[0], m1_ref[0])
    a1f, a2f = dec_ref[0, 0:1, :], dec_ref[0, 1:2, :]
    a1b, a2b = dec_ref[0, 2:3, :], dec_ref[0, 3:4, :]
    zero = jnp.zeros((n_batch, sw), F32)

    def step(j, carry):
        hf, hfs, hb, hbs = carry
        cb = jnp.where(j < n_ctx_chunks, n_ctx_chunks - 1 - j, n_chunks - 1 + n_ctx_chunks - j)
        rf = pl.multiple_of(j * n_batch, n_batch)
        rb = pl.multiple_of(cb * n_batch, n_batch)
        h_ref[pl.ds(rf, n_batch), 0:sw] = hf
        h_ref[pl.ds(rb, n_batch), sw:2 * sw] = hb
        sf = p_ref[pl.ds(rf, n_batch), k:k + sw]
        sfs = p_ref[pl.ds(rf, n_batch), k + sw:k + 2 * sw]
        sb = p_ref[pl.ds(rb, n_batch), k + 2 * sw:k + 3 * sw]
        sbs = p_ref[pl.ds(rb, n_batch), k + 3 * sw:k + 4 * sw]
        return (a1f * hf + a2f * hfs + sf, a1f * hfs - a2f * hf + sfs,
                a1b * hb + a2b * hbs + sb, a1b * hbs - a2b * hb + sbs)

    lax.fori_loop(0, n_chunks, step, (zero, zero, zero, zero))
    inter = _dot(h_ref[...].astype(BF16), m2_ref[0])
    o_ref[0] = (p_ref[:, 0:k] + inter).astype(BF16)


def _s5_mixer(x_blocks, m1, m2, dec, n_batch, n_ctx_chunks):
    g, r, k = x_blocks.shape
    n_chunks = r // n_batch
    kern = functools.partial(_s5_kernel, n_batch=n_batch, n_ctx_chunks=n_ctx_chunks, n_chunks=n_chunks)
    return pl.pallas_call(
        kern,
        grid=(g,),
        in_specs=[pl.BlockSpec((1, r, k), lambda i: (i, 0, 0)),
                  pl.BlockSpec((1, k, m1.shape[2]), lambda i: (i, 0, 0)),
                  pl.BlockSpec((1, m2.shape[1], k), lambda i: (i, 0, 0)),
                  pl.BlockSpec((1, 4, 2 * SSM_STATE), lambda i: (i, 0, 0))],
        out_specs=pl.BlockSpec((1, r, k), lambda i: (i, 0, 0)),
        out_shape=jax.ShapeDtypeStruct((g, r, k), BF16),
        scratch_shapes=[pltpu.VMEM((r, m1.shape[2]), F32), pltpu.VMEM((r, 4 * SSM_STATE), F32)],
        compiler_params=_cparams(("parallel",)),
        name="s5_mixer",
    )(x_blocks, m1, m2, dec)


def _qk_conv_silu(x, w, b, scale, lc):
    lt = x.shape[0]
    row = lax.broadcasted_iota(jnp.int32, (lt, 1), 0)
    prev = jnp.where((row == 0) | (row == lc), 0.0, pltpu.roll(x, 1, axis=0))
    nxt = jnp.where((row == lc - 1) | (row == lt - 1), 0.0, pltpu.roll(x, lt - 1, axis=0))
    y = w[0:1, :] * prev + w[1:2, :] * x + w[2:3, :] * nxt + b
    return y * jax.nn.sigmoid(y) * scale


def _log_sigmoid(x):
    return jnp.minimum(x, 0.0) - jnp.log1p(jnp.exp(-jnp.abs(x)))


def _mlstm_masks():
    t, nv = MLSTM_CHUNK, 2 * N_MLSTM_HEADS
    r_i = jnp.arange(t)[:, None]
    c_i = jnp.arange(t)[None, :]
    cum = jnp.stack([r_i <= c_i, r_i >= c_i]).astype(BF16)
    neg = jnp.where(jnp.stack([c_i <= r_i, c_i >= r_i]), 0.0, -jnp.inf).astype(F32)
    sel = (jnp.arange(nv * LANES)[None, :] // LANES == jnp.arange(nv)[:, None]).astype(BF16)
    return cum, neg, sel


def _mlstm_chunk(q, k, v, ig, f_row, f_col, i_col, neg, cn0, m0, rev):
    t, dh = MLSTM_CHUNK, MLSTM_HEAD
    ones = jnp.ones((t, dh), BF16)
    r_row = f_row - ig
    f_end = f_row[:, 0:1] if rev else f_row[:, t - 1:t]
    log_d = f_col - r_row + neg
    log_inter = f_col + m0
    m_t = jnp.maximum(jnp.max(log_d, axis=1, keepdims=True), log_inter)
    s_qk = (_nt_dot(q, k) * jnp.exp(log_d - m_t)).astype(BF16)
    e_inter = jnp.exp(log_inter - m_t)
    intra = _dot(s_qk, jnp.concatenate([v, ones], axis=1))
    inter = _dot(q, cn0.astype(BF16))
    num = intra[:, 0:dh] + e_inter * inter[:, 0:dh]
    den = intra[:, dh:2 * dh] + e_inter * inter[:, dh:2 * dh]
    h = num / jnp.maximum(jnp.abs(den), jnp.exp(-m_t))
    m_loc = f_end - jnp.min(r_row, axis=1, keepdims=True)
    e_end = jnp.exp(f_end - (f_col - i_col) - m_loc)
    w = jnp.concatenate([e_end * v.astype(F32), e_end], axis=1).astype(BF16)
    d_cn = _tn_dot(k, w)
    m_new = jnp.maximum(f_end + m0, m_loc)
    a = jnp.exp(f_end + m0 - m_new)
    bb = jnp.exp(m_loc - m_new)
    return h, a * cn0 + bb * d_cn, m_new


def _mlstm_kernel(q_ref, k_ref, v_ref, o_ref, gt_ref, gb_ref, g_ref, cum_ref, neg_ref, sel_ref,
                  cw_ref, cb_ref, cs_ref, out_ref, acc_ref, cn_ref, m_ref, qk_ref, *, n_ctx_chunks, n_chunks):
    t, dh, nh = MLSTM_CHUNK, MLSTM_HEAD, N_MLSTM_HEADS
    acc_ref[...] = jnp.zeros_like(acc_ref)
    cn_ref[...] = jnp.zeros_like(cn_ref)
    m_ref[...] = jnp.zeros_like(m_ref)
    slab = 2 * LANES
    for half, src in enumerate((q_ref, k_ref)):
        for c0 in range(0, D_MLSTM, slab):
            dst = slice(half * D_MLSTM + c0, half * D_MLSTM + c0 + slab)
            qk_ref[:, dst] = _qk_conv_silu(src[0, :, c0:c0 + slab].astype(F32), cw_ref[:, dst], cb_ref[:, dst],
                                           cs_ref[:, dst], n_ctx_chunks * t).astype(BF16)

    def step(j, carry):
        for d in range(2):
            rev = d == 1
            if rev:
                c = jnp.where(j < n_ctx_chunks, n_ctx_chunks - 1 - j, n_chunks - 1 + n_ctx_chunks - j)
            else:
                c = j
            r0 = pl.multiple_of(c * t, t)
            gates = gt_ref[0, :, pl.ds(r0, t)] + gb_ref[...]
            ig = gates[2 * nh * d:2 * nh * d + nh]
            lf = _log_sigmoid(gates[2 * nh * d + nh:2 * nh * (d + 1)])
            f_rows = _dot_split(lf, cum_ref[d])
            rows = jnp.concatenate([f_rows, ig], axis=0)
            hi = rows.astype(BF16)
            lo = (rows - hi.astype(F32)).astype(BF16)
            cols = _tn_dot(hi, sel_ref[...]) + _tn_dot(lo, sel_ref[...])
            for hd in range(nh):
                idx = d * nh + hd
                cs = slice(hd * dh, (hd + 1) * dh)
                ks = slice(D_MLSTM + hd * dh, D_MLSTM + (hd + 1) * dh)
                h, cn, m = _mlstm_chunk(qk_ref[pl.ds(r0, t), cs], qk_ref[pl.ds(r0, t), ks],
                                        v_ref[0, pl.ds(r0, t), cs], ig[hd:hd + 1], f_rows[hd:hd + 1],
                                        cols[:, hd * dh:(hd + 1) * dh], cols[:, (nh + hd) * dh:(nh + hd + 1) * dh],
                                        neg_ref[d], cn_ref[idx], m_ref[idx:idx + 1, 0:1], rev)
                acc_ref[pl.ds(r0, t), cs] += h
                cn_ref[idx] = cn
                m_ref[idx:idx + 1, :] = jnp.broadcast_to(m, (1, LANES))
        return carry

    lax.fori_loop(0, n_chunks, step, 0)

    def finish(c, carry):
        r0 = pl.multiple_of(c * t, t)
        for hd in range(nh):
            cs = slice(hd * dh, (hd + 1) * dh)
            hsum = acc_ref[pl.ds(r0, t), cs]
            y = hsum * lax.rsqrt(jnp.mean(hsum * hsum, axis=-1, keepdims=True) + EPS) * g_ref[:, cs]
            out_ref[0, pl.ds(r0, t), cs] = (y * o_ref[0, pl.ds(r0, t), cs].astype(F32)).astype(BF16)
        return carry

    lax.fori_loop(0, n_chunks, finish, 0)


def _mlstm(proj, gates_t, gate_b, norm_g, conv_w, conv_b, conv_scale, col_qk, col_v, col_o, lc):
    b, lt, _ = proj.shape
    w, t = D_MLSTM, MLSTM_CHUNK
    kern = functools.partial(_mlstm_kernel, n_ctx_chunks=lc // t, n_chunks=lt // t)
    cum, neg, sel = _mlstm_masks()
    return pl.pallas_call(
        kern,
        grid=(b,),
        in_specs=[pl.BlockSpec((1, lt, w), lambda bi: (bi, 0, col_qk // w)),
                  pl.BlockSpec((1, lt, w), lambda bi: (bi, 0, col_qk // w + 1)),
                  pl.BlockSpec((1, lt, w), lambda bi: (bi, 0, col_v // w)),
                  pl.BlockSpec((1, lt, w), lambda bi: (bi, 0, col_o // w)),
                  pl.BlockSpec((1, N_GATES, lt), lambda bi: (bi, 0, 0)),
                  pl.BlockSpec((N_GATES, 1), lambda bi: (0, 0)),
                  pl.BlockSpec((1, w), lambda bi: (0, 0)),
                  pl.BlockSpec(cum.shape, lambda bi: (0, 0, 0)),
                  pl.BlockSpec(neg.shape, lambda bi: (0, 0, 0)),
                  pl.BlockSpec(sel.shape, lambda bi: (0, 0)),
                  pl.BlockSpec((QK_CONV, 2 * w), lambda bi: (0, 0)),
                  pl.BlockSpec((1, 2 * w), lambda bi: (0, 0)),
                  pl.BlockSpec((1, 2 * w), lambda bi: (0, 0))],
        out_specs=pl.BlockSpec((1, lt, w), lambda bi: (bi, 0, 0)),
        out_shape=jax.ShapeDtypeStruct((b, lt, w), BF16),
        scratch_shapes=[pltpu.VMEM((lt, w), F32),
                        pltpu.VMEM((2 * N_MLSTM_HEADS, MLSTM_HEAD, 2 * MLSTM_HEAD), F32),
                        pltpu.VMEM((2 * N_MLSTM_HEADS, LANES), F32),
                        pltpu.VMEM((lt, 2 * w), BF16)],
        compiler_params=_cparams(("parallel",)),
        name="mlstm_mixer",
    )(proj, proj, proj, proj, gates_t, gate_b, norm_g.reshape(1, w), cum, neg, sel,
      conv_w, conv_b.reshape(1, 2 * w), conv_scale)


def _attn_kernel(sc_ref, q_ref, k_ref, v_ref, lam_ref, g_ref, o_ref, vext_ref, *, lc, ctx_tiles):
    i = pl.program_id(2)
    dv = DIFF_V_HEAD
    lp = lam_ref[...]
    lam = (jnp.exp(jnp.sum(lp[0:1] * lp[1:2], axis=1, keepdims=True))
           - jnp.exp(jnp.sum(lp[2:3] * lp[3:4], axis=1, keepdims=True)) + sc_ref[0])
    lane = lax.broadcasted_iota(jnp.int32, (1, dv), 1)

    n_heads = q_ref.shape[2] // dv

    @pl.when(i == 0)
    def _():
        for h in range(n_heads):
            vext_ref[:, 2 * h * dv:(2 * h + 1) * dv] = v_ref[0, :, h * dv:(h + 1) * dv]
            vext_ref[:, (2 * h + 1) * dv:(2 * h + 2) * dv] = jnp.ones((v_ref.shape[1], dv), BF16)

    def attend(nk, kc):
        for h in range(n_heads):
            hs = slice(h * dv, (h + 1) * dv)
            q = q_ref[0, :, hs]
            outs = []
            for c in range(2):
                keep = (lane < DIFF_QK_HEAD) if c == 0 else (lane >= DIFF_QK_HEAD)
                qc = jnp.where(keep, q, jnp.zeros_like(q))
                m_run, acc = None, None
                for k0 in range(0, nk, kc):
                    s = _nt_dot(qc, k_ref[0, k0:k0 + kc, hs])
                    m_j = jnp.max(s, axis=1, keepdims=True)
                    m_new = m_j if m_run is None else jnp.maximum(m_run, m_j)
                    pv = _dot(jnp.exp(s - m_new).astype(BF16), vext_ref[k0:k0 + kc, 2 * h * dv:(2 * h + 2) * dv])
                    acc = pv if m_run is None else acc * jnp.exp(m_run - m_new) + pv
                    m_run = m_new
                outs.append(acc[:, 0:dv] / acc[:, dv:2 * dv])
            o = outs[0] - lam * outs[1]
            y = o * lax.rsqrt(jnp.mean(o * o, axis=-1, keepdims=True) + EPS) * g_ref[...]
            o_ref[0, :, hs] = (y * sc_ref[1]).astype(BF16)

    @pl.when(i < ctx_tiles)
    def _():
        attend(lc, lc)

    @pl.when(i >= ctx_tiles)
    def _():
        nk = k_ref.shape[1]
        attend(nk, ATTN_KEY_CHUNK if nk % ATTN_KEY_CHUNK == 0 else nk)


def _attention(scal, proj, lam_p, subln_g, col_q, col_k, col_v, lc):
    b, lt, _ = proj.shape
    w, dv = D_DIFF, DIFF_V_HEAD
    kern = functools.partial(_attn_kernel, lc=lc, ctx_tiles=lc // ROW_TILE)
    return pl.pallas_call(
        kern,
        grid=(b, 1, lt // ROW_TILE),
        in_specs=[pl.BlockSpec(memory_space=pltpu.SMEM),
                  pl.BlockSpec((1, ROW_TILE, w), lambda bi, hi, i: (bi, i, col_q // w)),
                  pl.BlockSpec((1, lt, w), lambda bi, hi, i: (bi, 0, col_k // w)),
                  pl.BlockSpec((1, lt, w), lambda bi, hi, i: (bi, 0, col_v // w)),
                  pl.BlockSpec((4, DIFF_QK_HEAD), lambda bi, hi, i: (0, 0)),
                  pl.BlockSpec((1, dv), lambda bi, hi, i: (0, 0))],
        out_specs=pl.BlockSpec((1, ROW_TILE, w), lambda bi, hi, i: (bi, i, 0)),
        out_shape=jax.ShapeDtypeStruct((b, lt, w), BF16),
        scratch_shapes=[pltpu.VMEM((lt, 2 * w), BF16)],
        compiler_params=_cparams(("parallel", "parallel", "arbitrary")),
        name="diff_attention",
    )(scal, proj, proj, proj, lam_p, subln_g.reshape(1, dv))


def _merge_kernel(x_ref, mod_ref, y5_ref, sz_ref, hb_ref, mz_ref, oc_ref, dz_ref, gl_ref,
                  gw_ref, gb_ref, wa_ref, wb_ref, wc_ref, wo_ref, o_ref, *, d_model):
    f = lambda r: r[0].astype(F32)
    rows = x_ref.shape[1]
    y = jnp.concatenate([y5_ref[j, :, 0].reshape(rows, LANES) for j in range(S5_SLABS)], axis=1).astype(F32)
    ge = 0.5 * y * (1.0 + jnp.tanh(math.sqrt(2.0 / math.pi) * (y + 0.044715 * (y * y * y))))
    t = _dot(ge.astype(BF16), gw_ref[...]) + gb_ref[...]
    ya = t[:, 0:D_SSM] * jax.nn.sigmoid(t[:, D_SSM:2 * D_SSM])
    pa = _dot((ya * f(sz_ref)).astype(BF16), wa_ref[...])
    pb = _dot((f(hb_ref) * f(mz_ref)).astype(BF16), wb_ref[...])
    pc = _dot((f(oc_ref) * f(dz_ref)).astype(BF16), wc_ref[...])
    gates = f(gl_ref)
    m = (gates[:, 0:d_model] * pa + gates[:, d_model:2 * d_model] * pb
         + gates[:, 2 * d_model:3 * d_model] * pc)
    out = _dot(m.astype(BF16), wo_ref[...])
    o_ref[0] = x_ref[0] + mod_ref[0, :, 2 * d_model:3 * d_model] * out


def _merge(x_all, mods3, y5, hb, oc, proj, glu_w, glu_b, wa, wb, wc, wo, cols, n_batch, lc, skip_ctx):
    b, lt, d = x_all.shape
    ctx_tiles = lc // ROW_TILE
    t0 = ctx_tiles if skip_ctx else 0
    w = D_SSM
    blk = lambda name: cols[name] // w
    blk_rows = ROW_TILE // S5_CHUNK
    tok = lambda j: pl.BlockSpec((1, ROW_TILE, w), lambda bi, i: (bi, i + t0, j))
    full = lambda a: pl.BlockSpec(a.shape, lambda bi, i: (0,) * a.ndim)
    kern = functools.partial(_merge_kernel, d_model=d)
    return pl.pallas_call(
        kern,
        grid=(b, lt // ROW_TILE - t0),
        in_specs=[pl.BlockSpec((1, ROW_TILE, d), lambda bi, i: (bi, i + t0, 0)),
                  pl.BlockSpec((1, 1, 3 * d), lambda bi, i: (jnp.where(i + t0 < ctx_tiles, n_batch, bi), 0, 0)),
                  pl.BlockSpec((S5_SLABS, blk_rows, 1, S5_CHUNK, LANES), lambda bi, i: (0, i + t0, bi, 0, 0)),
                  tok(blk("sz")), tok(0), tok(blk("mz")), tok(0), tok(blk("dz")),
                  pl.BlockSpec((1, ROW_TILE, N_BRANCH * d), lambda bi, i: (bi, i + t0, 0)),
                  full(glu_w), full(glu_b), full(wa), full(wb), full(wc), full(wo)],
        out_specs=pl.BlockSpec((1, ROW_TILE, d), lambda bi, i: (bi, i, 0)),
        out_shape=jax.ShapeDtypeStruct((b, lt - t0 * ROW_TILE, d), F32),
        compiler_params=_cparams(("parallel", "parallel")),
        name="merge_outproj",
    )(x_all, mods3, y5, proj, hb, proj, oc, proj, proj, glu_w, glu_b, wa, wb, wc, wo)


def _rope_tables(lc, l):
    half = DIFF_QK_HEAD // 2
    rows = l // GRID_W
    row = jnp.repeat(jnp.arange(rows), GRID_W).astype(F32)
    col = jnp.tile(jnp.arange(GRID_W), rows).astype(F32)
    inv = jnp.power(ROPE_BASE, -jnp.arange(0, half, 2, dtype=F32) / half)
    ang_r, ang_c = row[:, None] * inv, col[:, None] * inv
    cos64 = jnp.concatenate([jnp.cos(ang_r), jnp.cos(ang_r), jnp.cos(ang_c), jnp.cos(ang_c)], axis=1)
    sin64 = jnp.concatenate([-jnp.sin(ang_r), jnp.sin(ang_r), -jnp.sin(ang_c), jnp.sin(ang_c)], axis=1)
    cos_t = jnp.concatenate([jnp.ones((lc, DIFF_QK_HEAD), F32), cos64], axis=0)
    sin_t = jnp.concatenate([jnp.zeros((lc, DIFF_QK_HEAD), F32), sin64], axis=0)
    return jnp.tile(cos_t, (1, 2)), jnp.tile(sin_t, (1, 2))


def kernel(x, c, ctx, c_ctx, norm_g, ada_w, ada_b, w_in, ssm_lam_re, ssm_lam_im, ssm_log_step, ssm_b_re, ssm_b_im, ssm_c_re, ssm_c_im, ssm_d, ssm_glu_w, ssm_glu_b, w_ssm_out, ml_conv_w, ml_conv_b, ml_gate_b, ml_norm_g, w_ml_out, da_qnorm_g, da_knorm_g, da_lambda, da_subln_g, w_da_out, w_out):
    n_batch, l, d = x.shape
    lc = ctx.shape[1]
    lt = lc + l
    depth = w_in.shape[0]
    cols = _cols(d)
    t5 = S5_CHUNK
    n_rows = (lt // t5) * n_batch
    assert lc % ROW_TILE == 0 and l % ROW_TILE == 0 and lc % MLSTM_CHUNK == 0 and n_batch % 8 == 0
    assert MLSTM_CHUNK == MLSTM_HEAD == LANES

    x_all = jnp.concatenate([ctx, x], axis=1)

    mod_rows = -(-(n_batch + 1) // 8) * 8
    cvec = jnp.zeros((mod_rows, d), F32).at[:n_batch].set(c).at[n_batch].set(c_ctx)
    mods = _modulation(cvec, ada_w, ada_b)

    g0 = 2 * D_SSM + 5 * D_MLSTM
    g1 = g0 + N_GATES
    m0 = g1 + 4 * D_DIFF
    w_p = jnp.concatenate([w_in[:, :, m0:], w_in[:, :, D_SSM:g0], w_in[:, :, g1:m0]], axis=2).astype(BF16)
    w_su = w_in[:, :, :D_SSM].astype(BF16)
    w_gt = jnp.swapaxes(w_in[:, :, g0:g1], 1, 2).astype(BF16)

    cos_t, sin_t = _rope_tables(lc, l)
    ones_bd = jnp.kron(jnp.eye(2 * LANES // DIFF_QK_HEAD, dtype=F32),
                       jnp.ones((DIFF_QK_HEAD, DIFF_QK_HEAD), F32)).astype(BF16)
    qk_scale = jnp.concatenate([jnp.ones((1, D_MLSTM), F32), jnp.full((1, D_MLSTM), MLSTM_HEAD ** -0.5, F32)], 1)
    perm = _s5_lane_permutation()
    s5_m1, s5_m2, s5_dec = jax.vmap(_s5_operators)(ssm_lam_re, ssm_lam_im, ssm_log_step, ssm_b_re, ssm_b_im,
                                                   ssm_c_re, ssm_c_im, ssm_d)

    for li in range(depth):
        last = li == depth - 1
        mods3 = mods[li].reshape(mod_rows, 1, 3 * d)
        qg = jnp.tile(da_qnorm_g[li].astype(F32), D_DIFF // DIFF_QK_HEAD).reshape(1, D_DIFF)
        kg = jnp.tile(da_knorm_g[li].astype(F32), D_DIFF // DIFF_QK_HEAD).reshape(1, D_DIFF)
        proj, su, gates_t = _inproj(x_all, mods3, norm_g[li], w_p[li], w_su[li], w_gt[li], cos_t, sin_t, qg, kg,
                                    ones_bd, cols, n_batch, lc)

        xb = _s5_gather(su.reshape(S5_SLABS, n_rows, S5_WIDE), perm)
        yb = _s5_mixer(xb, s5_m1[li], s5_m2[li], s5_dec[li], n_batch, lc // t5)
        y5 = _s5_scatter(yb, perm).reshape(su.shape)

        hb = _mlstm(proj, gates_t, ml_gate_b[li].astype(F32).reshape(N_GATES, 1), ml_norm_g[li],
                    ml_conv_w[li], ml_conv_b[li], qk_scale, cols["mqk"], cols["mv"], cols["mo"], lc)

        lam_init = 0.8 - 0.6 * math.exp(-0.3 * li)
        scal = jnp.array([lam_init, 1.0 - lam_init], F32)
        oc = _attention(scal, proj, da_lambda[li].astype(F32), da_subln_g[li].astype(F32),
                        cols["dq"], cols["dk"], cols["dv"], lc)

        x_all = _merge(x_all, mods3, y5, hb, oc, proj, ssm_glu_w[li].astype(BF16),
                       ssm_glu_b[li].astype(F32).reshape(1, -1), w_ssm_out[li].astype(BF16),
                       w_ml_out[li].astype(BF16), w_da_out[li].astype(BF16), w_out[li].astype(BF16),
                       cols, n_batch, lc, skip_ctx=last)
    return x_all
```

```python
import functools
import math

import jax
import jax.numpy as jnp
from jax import lax
from jax.experimental import pallas as pl
from jax.experimental.pallas import tpu as pltpu

F32 = jnp.float32
BF16 = jnp.bfloat16
HIGHEST = lax.Precision.HIGHEST

EPS = 1e-6
GRID_W = 64
ROPE_BASE = 10000.0
LANES = 128

D_SSM = 512
SSM_GROUP = 16
N_SSM_GROUPS = D_SSM // SSM_GROUP
SSM_STATE = 64
S5_CHUNK = 16
S5_K = S5_CHUNK * SSM_GROUP
S5_LANE_GROUPS = LANES // SSM_GROUP
S5_SLABS = D_SSM // LANES
S5_WIDE = S5_CHUNK * LANES
S5_ROW_TILES = (768, 512, 256)
D_MLSTM = 512
N_MLSTM_HEADS = 4
MLSTM_HEAD = D_MLSTM // N_MLSTM_HEADS
MLSTM_CHUNK = 128
QK_CONV = 3
N_GATES = 4 * N_MLSTM_HEADS
N_DIFF_HEADS = 4
DIFF_QK_HEAD = 64
DIFF_V_HEAD = 2 * DIFF_QK_HEAD
D_DIFF = N_DIFF_HEADS * DIFF_V_HEAD
ATTN_KEY_CHUNK = 768
N_BRANCH = 3

ROW_TILE = 256
VMEM_LIMIT = 56 * 1024 * 1024


def _cols(d_model):
    off = {"gl": 0}
    o = N_BRANCH * d_model
    for name, w in (("sz", D_SSM), ("mqk", 2 * D_MLSTM), ("mv", D_MLSTM), ("mo", D_MLSTM),
                    ("mz", D_MLSTM), ("dq", D_DIFF), ("dk", D_DIFF), ("dv", D_DIFF), ("dz", D_DIFF)):
        off[name] = o
        o += w
    off["total"] = o
    return off


def _cparams(sem):
    return pltpu.CompilerParams(dimension_semantics=sem, vmem_limit_bytes=VMEM_LIMIT)


def _nt_dot(a, b):
    return lax.dot_general(a, b, (((1,), (1,)), ((), ())), preferred_element_type=F32)


def _tn_dot(a, b):
    return lax.dot_general(a, b, (((0,), (0,)), ((), ())), preferred_element_type=F32)


def _dot(a, b):
    return jnp.dot(a, b, preferred_element_type=F32)


def _dot_split(x, w):
    hi = x.astype(BF16)
    lo = (x - hi.astype(F32)).astype(BF16)
    return _dot(hi, w) + _dot(lo, w)


def _mods_kernel(c_ref, w_ref, b_ref, o_ref):
    c = c_ref[...]
    act = c * jax.nn.sigmoid(c)
    o_ref[0] = jnp.dot(act, w_ref[0], preferred_element_type=F32, precision=HIGHEST) + b_ref[0]


def _modulation(cvec, ada_w, ada_b):
    depth, d, d3 = ada_w.shape
    rows = cvec.shape[0]
    return pl.pallas_call(
        _mods_kernel,
        grid=(depth, d3 // d),
        in_specs=[pl.BlockSpec((rows, d), lambda l, j: (0, 0)),
                  pl.BlockSpec((1, d, d), lambda l, j: (l, 0, j)),
                  pl.BlockSpec((1, 1, d), lambda l, j: (l, 0, j))],
        out_specs=pl.BlockSpec((1, rows, d), lambda l, j: (l, 0, j)),
        out_shape=jax.ShapeDtypeStruct((depth, rows, d3), F32),
        compiler_params=_cparams(("arbitrary", "arbitrary")),
        name="adaln_modulation",
    )(cvec, ada_w, ada_b.reshape(depth, 1, d3))


def _qk_norm_rope(x, gain, cos, sin, ones_bd, scale):
    half = ones_bd.shape[0]
    ms = jnp.concatenate([_dot_split(x[:, c0:c0 + half] * x[:, c0:c0 + half], ones_bd)
                          for c0 in range(0, x.shape[1], half)], axis=1) * (1.0 / DIFF_QK_HEAD)
    y = x * lax.rsqrt(ms + EPS) * gain
    lane = lax.broadcasted_iota(jnp.int32, (1, LANES), 1)
    first_half = (lane // 16) % 2 == 0
    outs = []
    for c0 in range(0, y.shape[1], LANES):
        yb = y[:, c0:c0 + LANES]
        partner = jnp.where(first_half, pltpu.roll(yb, LANES - 16, axis=1), pltpu.roll(yb, 16, axis=1))
        outs.append(yb * cos + partner * sin)
    return jnp.concatenate(outs, axis=1) * scale


def _inproj_kernel(x_ref, mod_ref, g_ref, w_ref, wsu_ref, wgt_ref, cos_ref, sin_ref, qg_ref, kg_ref, ones_ref,
                   o_ref, su_ref, ogt_ref, *, d_model, col_chunk, col_q, col_k, sigmoid_cols, silu_cols):
    x = x_ref[0]
    ms = jnp.mean(x * x, axis=-1, keepdims=True)
    y = x * lax.rsqrt(ms + EPS) * g_ref[...]
    sh = mod_ref[0, :, 0:d_model]
    sc = mod_ref[0, :, d_model:2 * d_model]
    hb = (y * (1.0 + sc) + sh).astype(BF16)
    n_out = o_ref.shape[-1]
    for c0 in range(0, n_out, col_chunk):
        acc = _dot(hb, w_ref[:, c0:c0 + col_chunk])
        if c0 == col_q:
            acc = _qk_norm_rope(acc, qg_ref[...], cos_ref[...], sin_ref[...], ones_ref[...], DIFF_QK_HEAD ** -0.5)
        elif c0 == col_k:
            acc = _qk_norm_rope(acc, kg_ref[...], cos_ref[...], sin_ref[...], ones_ref[...], 1.0)
        elif c0 in sigmoid_cols:
            acc = jax.nn.sigmoid(acc)
        elif c0 in silu_cols:
            acc = acc * jax.nn.sigmoid(acc)
        o_ref[0, :, c0:c0 + col_chunk] = acc.astype(BF16)
    rows = x.shape[0]
    for j in range(S5_SLABS):
        su = _dot(hb, wsu_ref[:, j * LANES:(j + 1) * LANES]).astype(BF16)
        su_ref[j, :, 0] = su.reshape(rows // S5_CHUNK, S5_CHUNK, LANES)
    ogt_ref[0] = _nt_dot(wgt_ref[...], hb)


def _inproj(x_all, mods3, norm_g, w_p, w_su, w_gt, cos_t, sin_t, qg, kg, ones_bd, cols, n_batch, lc):
    b, lt, d = x_all.shape
    n_out = w_p.shape[1]
    ctx_tiles = lc // ROW_TILE
    chunk = D_DIFF
    assert all(cols[k] % chunk == 0 for k in ("dq", "dk", "sz", "mo", "mz", "dz")) and cols["sz"] == N_BRANCH * d
    sigmoid_cols = tuple(range(cols["gl"], cols["sz"], chunk)) + (cols["mo"],)
    silu_cols = (cols["sz"], cols["mz"], cols["dz"])
    kern = functools.partial(_inproj_kernel, d_model=d, col_chunk=chunk, col_q=cols["dq"], col_k=cols["dk"],
                             sigmoid_cols=sigmoid_cols, silu_cols=silu_cols)
    blk_rows = ROW_TILE // S5_CHUNK
    return pl.pallas_call(
        kern,
        grid=(b, lt // ROW_TILE),
        in_specs=[pl.BlockSpec((1, ROW_TILE, d), lambda bi, i: (bi, i, 0)),
                  pl.BlockSpec((1, 1, 3 * d), lambda bi, i: (jnp.where(i < ctx_tiles, n_batch, bi), 0, 0)),
                  pl.BlockSpec((1, d), lambda bi, i: (0, 0)),
                  pl.BlockSpec((d, n_out), lambda bi, i: (0, 0), pipeline_mode=pl.Buffered(1)),
                  pl.BlockSpec((d, D_SSM), lambda bi, i: (0, 0)),
                  pl.BlockSpec((N_GATES, d), lambda bi, i: (0, 0)),
                  pl.BlockSpec((ROW_TILE, LANES), lambda bi, i: (i, 0)),
                  pl.BlockSpec((ROW_TILE, LANES), lambda bi, i: (i, 0)),
                  pl.BlockSpec((1, D_DIFF), lambda bi, i: (0, 0)),
                  pl.BlockSpec((1, D_DIFF), lambda bi, i: (0, 0)),
                  pl.BlockSpec(ones_bd.shape, lambda bi, i: (0, 0))],
        out_specs=[pl.BlockSpec((1, ROW_TILE, n_out), lambda bi, i: (bi, i, 0)),
                   pl.BlockSpec((S5_SLABS, blk_rows, 1, S5_CHUNK, LANES), lambda bi, i: (0, i, bi, 0, 0)),
                   pl.BlockSpec((1, N_GATES, ROW_TILE), lambda bi, i: (bi, 0, i))],
        out_shape=[jax.ShapeDtypeStruct((b, lt, n_out), BF16),
                   jax.ShapeDtypeStruct((S5_SLABS, lt // S5_CHUNK, b, S5_CHUNK, LANES), BF16),
                   jax.ShapeDtypeStruct((b, N_GATES, lt), F32)],
        compiler_params=_cparams(("parallel", "parallel")),
        name="adaln_inproj",
    )(x_all, mods3, norm_g.reshape(1, d), w_p, w_su, w_gt, cos_t, sin_t, qg, kg, ones_bd)


def _s5_ops_kernel(lamc_ref, lamr_ref, step_ref, btre_ref, btim_ref, ctre_ref, ctim_ref, drep_ref,
                   m1_ref, m2_ref, dec_ref):
    t, n, p, k = S5_CHUNK, SSM_GROUP, SSM_STATE, S5_K
    hp = HIGHEST
    lane = lax.broadcasted_iota(jnp.int32, (1, k), 1)
    tok = lane % t
    tokf = tok.astype(F32)
    expand = (lax.broadcasted_iota(jnp.int32, (n, k), 0) == lane // t).astype(F32)
    pow_idx = lax.broadcasted_iota(jnp.int32, (24, 1), 0).astype(F32)
    tin_rows = [None] * t
    states, readouts, decs = [], [], []
    for d in range(2):
        dt = jnp.exp(step_ref[0, 0, :, d:d + 1])
        lre_c = jnp.minimum(lamc_ref[0, 0, :, 2 * d:2 * d + 1], -1e-4)
        lim_c = lamc_ref[0, 0, :, 2 * d + 1:2 * d + 2]
        lre_r = jnp.minimum(lamr_ref[0, 0, 2 * d:2 * d + 1, :], -1e-4)
        lim_r = lamr_ref[0, 0, 2 * d + 1:2 * d + 2, :]
        ldre_c, ldim_c = lre_c * dt, lim_c * dt
        ldre_r, ldim_r = lre_r * dt, lim_r * dt
        mag = jnp.exp(ldre_r)
        nre, nim = mag * jnp.cos(ldim_r) - 1.0, mag * jnp.sin(ldim_r)
        den = lre_r * lre_r + lim_r * lim_r
        coef_re = (nre * lre_r + nim * lim_r) / den
        coef_im = (nim * lre_r - nre * lim_r) / den
        btre, btim = btre_ref[0, d, 0], btim_ref[0, d, 0]
        bb_re = coef_re * btre - coef_im * btim
        bb_im = coef_re * btim + coef_im * btre
        ct_re = jnp.dot(ctre_ref[0, d, 0], expand, preferred_element_type=F32, precision=hp)
        ct_im = jnp.dot(ctim_ref[0, d, 0], expand, preferred_element_type=F32, precision=hp)

        def c_times_power(kvec):
            pm = jnp.exp(kvec * ldre_c)
            pr, pi = pm * jnp.cos(kvec * ldim_c), pm * jnp.sin(kvec * ldim_c)
            return ct_re * pr - ct_im * pi, ct_re * pi + ct_im * pr

        q_re, q_im = c_times_power(tokf if d == 0 else (t - 1.0) - tokf)
        v = (jnp.dot(bb_re, q_re, preferred_element_type=F32, precision=hp)
             - jnp.dot(bb_im, q_im, preferred_element_type=F32, precision=hp))
        for a in range(t):
            shift = a if d == 0 else (a + k - (t - 1)) % k
            blk = pltpu.roll(v, shift, axis=1) if shift else v
            blk = jnp.where((tok >= a) if d == 0 else (tok <= a), blk, 0.0)
            tin_rows[a] = blk if tin_rows[a] is None else tin_rows[a] + blk
        pmag = jnp.exp(pow_idx * ldre_r)
        pw_re, pw_im = pmag * jnp.cos(pow_idx * ldim_r), pmag * jnp.sin(pow_idx * ldim_r)
        ri, ir = [], []
        for a in range(t):
            kx = t - 1 - a if d == 0 else a
            pr, pi = pw_re[kx:kx + 1], pw_im[kx:kx + 1]
            s_re, s_im = bb_re * pr - bb_im * pi, bb_re * pi + bb_im * pr
            ri.append(jnp.concatenate([s_re, s_im], axis=1))
            ir.append(jnp.concatenate([s_im, s_re], axis=1))
        states += [jnp.concatenate(ri, axis=0), jnp.concatenate(ir, axis=0)]
        ca_re, ca_im = c_times_power(tokf + 1.0 if d == 0 else t - tokf)
        readouts += [ca_re, -ca_im]
        decs += [jnp.concatenate([pw_re[t:t + 1], pw_re[t:t + 1]], axis=1),
                 jnp.concatenate([-pw_im[t:t + 1], pw_im[t:t + 1]], axis=1)]
    row = lax.broadcasted_iota(jnp.int32, (k, 1), 0)
    skip = jnp.where(lane == (row % n) * t + row // n, drep_ref[0, 0], 0.0)
    tin = jnp.concatenate(tin_rows, axis=0) + skip
    m1_ref[0, 0] = jnp.concatenate([tin] + states, axis=1).astype(BF16)
    m2_ref[0, 0] = jnp.concatenate(readouts, axis=0).astype(BF16)
    dec_ref[0, 0] = jnp.concatenate(decs, axis=0)


def _s5_block_operators(lam_re, lam_im, log_step, b_re, b_im, c_re, c_im, d_skip):
    depth = lam_re.shape[0]
    g, p, n, k = N_SSM_GROUPS, SSM_STATE, SSM_GROUP, S5_K
    f = lambda a: a.astype(F32)
    lam = jnp.stack([f(lam_re[:, 0]), f(lam_im[:, 0]), f(lam_re[:, 1]), f(lam_im[:, 1])], axis=-1)
    step = jnp.swapaxes(f(log_step), 1, 2).reshape(depth, g, 1, 2)
    d_rep = jnp.tile(f(d_skip).reshape(depth, g, 1, n), (1, 1, S5_CHUNK, 1)).reshape(depth, g, k, 1)
    per = lambda shape: pl.BlockSpec((1, 1) + shape, lambda l, i: (l, i) + (0,) * len(shape))
    both = lambda shape: pl.BlockSpec((1, 2, 1) + shape, lambda l, i: (l, 0, i) + (0,) * len(shape))
    sw = 2 * p
    return pl.pallas_call(
        _s5_ops_kernel,
        grid=(depth, g),
        in_specs=[per((p, 4)), per((4, p)), per((1, 2)), both((n, p)), both((n, p)), both((p, n)), both((p, n)),
                  per((k, 1))],
        out_specs=[per((k, k + 4 * sw)), per((2 * sw, k)), per((4, sw))],
        out_shape=[jax.ShapeDtypeStruct((depth, g, k, k + 4 * sw), BF16),
                   jax.ShapeDtypeStruct((depth, g, 2 * sw, k), BF16),
                   jax.ShapeDtypeStruct((depth, g, 4, sw), F32)],
        compiler_params=_cparams(("parallel", "parallel")),
        name="s5_block_operators",
    )(lam, jnp.swapaxes(lam, 2, 3), step, jnp.swapaxes(f(b_re), 3, 4), jnp.swapaxes(f(b_im), 3, 4),
      jnp.swapaxes(f(c_re), 3, 4), jnp.swapaxes(f(c_im), 3, 4), d_rep)


def _s5_lane_permutations():
    src = jnp.arange(S5_WIDE)
    s, rem = src // LANES, src % LANES
    g8, n = rem // SSM_GROUP, rem % SSM_GROUP
    cols = jnp.arange(S5_WIDE)[None, :]
    p_in = ((g8 * S5_K + s * SSM_GROUP + n)[:, None] == cols).astype(BF16)
    p_out = ((g8 * S5_K + n * S5_CHUNK + s)[:, None] == cols).astype(BF16)
    return p_in, p_out


def _s5_gather_kernel(l_ref, p_ref, o_ref):
    res = _dot(l_ref[0], p_ref[...])
    for g8 in range(S5_LANE_GROUPS):
        o_ref[g8] = res[:, g8 * S5_K:(g8 + 1) * S5_K].astype(BF16)


def _s5_row_tile(n_rows):
    return next(t for t in S5_ROW_TILES if n_rows % t == 0)


def _s5_gather(su_wide, perm):
    slabs, r, wide = su_wide.shape
    rt = _s5_row_tile(r)
    return pl.pallas_call(
        _s5_gather_kernel,
        grid=(slabs, r // rt),
        in_specs=[pl.BlockSpec((1, rt, wide), lambda j, i: (j, i, 0)),
                  pl.BlockSpec((wide, wide), lambda j, i: (0, 0), pipeline_mode=pl.Buffered(1))],
        out_specs=pl.BlockSpec((S5_LANE_GROUPS, rt, S5_K), lambda j, i: (j, i, 0)),
        out_shape=jax.ShapeDtypeStruct((slabs * S5_LANE_GROUPS, r, S5_K), BF16),
        compiler_params=_cparams(("parallel", "parallel")),
        name="s5_gather",
    )(su_wide, perm)


def _s5_scatter_kernel(y_ref, p_ref, o_ref):
    y = jnp.concatenate([y_ref[g8] for g8 in range(S5_LANE_GROUPS)], axis=1)
    o_ref[0] = _nt_dot(y, p_ref[...]).astype(BF16)


def _s5_scatter(y_blocks, perm):
    g, r, k = y_blocks.shape
    slabs = g // S5_LANE_GROUPS
    rt = _s5_row_tile(r)
    return pl.pallas_call(
        _s5_scatter_kernel,
        grid=(slabs, r // rt),
        in_specs=[pl.BlockSpec((S5_LANE_GROUPS, rt, k), lambda j, i: (j, i, 0)),
                  pl.BlockSpec((S5_WIDE, S5_WIDE), lambda j, i: (0, 0), pipeline_mode=pl.Buffered(1))],
        out_specs=pl.BlockSpec((1, rt, S5_WIDE), lambda j, i: (j, i, 0)),
        out_shape=jax.ShapeDtypeStruct((slabs, r, S5_WIDE), BF16),
        compiler_params=_cparams(("parallel", "parallel")),
        name="s5_scatter",
    )(y_blocks, perm)


def _s5_kernel(x_ref, m1_ref, m2_ref, dec_ref, o_ref, p_ref, h_ref, *, n_batch, n_ctx_chunks, n_chunks):
    k, sw = S5_K, 2 * SSM_STATE
    p_ref[...] = _dot(x_ref[0], m1_ref[0])
    a1f, a2f = dec_ref[0, 0:1, :], dec_ref[0, 1:2, :]
    a1b, a2b = dec_ref[0, 2:3, :], dec_ref[0, 3:4, :]
    zero = jnp.zeros((n_batch, sw), F32)

    def step(j, carry):
        hf, hfs, hb, hbs = carry
        cb = jnp.where(j < n_ctx_chunks, n_ctx_chunks - 1 - j, n_chunks - 1 + n_ctx_chunks - j)
        rf = pl.multiple_of(j * n_batch, n_batch)
        rb = pl.multiple_of(cb * n_batch, n_batch)
        h_ref[pl.ds(rf, n_batch), 0:sw] = hf
        h_ref[pl.ds(rb, n_batch), sw:2 * sw] = hb
        sf = p_ref[pl.ds(rf, n_batch), k:k + sw]
        sfs = p_ref[pl.ds(rf, n_batch), k + sw:k + 2 * sw]
        sb = p_ref[pl.ds(rb, n_batch), k + 2 * sw:k + 3 * sw]
        sbs = p_ref[pl.ds(rb, n_batch), k + 3 * sw:k + 4 * sw]
        return (a1f * hf + a2f * hfs + sf, a1f * hfs - a2f * hf + sfs,
                a1b * hb + a2b * hbs + sb, a1b * hbs - a2b * hb + sbs)

    lax.fori_loop(0, n_chunks, step, (zero, zero, zero, zero))
    inter = _dot(h_ref[...].astype(BF16), m2_ref[0])
    o_ref[0] = (p_ref[:, 0:k] + inter).astype(BF16)


def _s5_mixer(x_blocks, m1, m2, dec, n_batch, n_ctx_chunks):
    g, r, k = x_blocks.shape
    n_chunks = r // n_batch
    kern = functools.partial(_s5_kernel, n_batch=n_batch, n_ctx_chunks=n_ctx_chunks, n_chunks=n_chunks)
    return pl.pallas_call(
        kern,
        grid=(g,),
        in_specs=[pl.BlockSpec((1, r, k), lambda i: (i, 0, 0)),
                  pl.BlockSpec((1, k, m1.shape[2]), lambda i: (i, 0, 0)),
                  pl.BlockSpec((1, m2.shape[1], k), lambda i: (i, 0, 0)),
                  pl.BlockSpec((1, 4, 2 * SSM_STATE), lambda i: (i, 0, 0))],
        out_specs=pl.BlockSpec((1, r, k), lambda i: (i, 0, 0)),
        out_shape=jax.ShapeDtypeStruct((g, r, k), BF16),
        scratch_shapes=[pltpu.VMEM((r, m1.shape[2]), F32), pltpu.VMEM((r, 4 * SSM_STATE), F32)],
        compiler_params=_cparams(("parallel",)),
        name="s5_mixer",
    )(x_blocks, m1, m2, dec)


def _qk_conv_silu(x, w, b, scale, lc):
    lt = x.shape[0]
    row = lax.broadcasted_iota(jnp.int32, (lt, 1), 0)
    prev = jnp.where((row == 0) | (row == lc), 0.0, pltpu.roll(x, 1, axis=0))
    nxt = jnp.where((row == lc - 1) | (row == lt - 1), 0.0, pltpu.roll(x, lt - 1, axis=0))
    y = w[0:1, :] * prev + w[1:2, :] * x + w[2:3, :] * nxt + b
    return y * jax.nn.sigmoid(y) * scale


def _log_sigmoid(x):
    return jnp.minimum(x, 0.0) - jnp.log1p(jnp.exp(-jnp.abs(x)))


def _mlstm_masks():
    t, nv = MLSTM_CHUNK, 2 * N_MLSTM_HEADS
    r_i = jnp.arange(t)[:, None]
    c_i = jnp.arange(t)[None, :]
    cum = jnp.stack([r_i <= c_i, r_i >= c_i]).astype(BF16)
    neg = jnp.where(jnp.stack([c_i <= r_i, c_i >= r_i]), 0.0, -jnp.inf).astype(F32)
    sel = (jnp.arange(nv * LANES)[None, :] // LANES == jnp.arange(nv)[:, None]).astype(BF16)
    return cum, neg, sel


def _mlstm_kernel(q_ref, k_ref, v_ref, o_ref, gt_ref, gb_ref, g_ref, cum_ref, neg_ref, sel_ref,
                  cw_ref, cb_ref, cs_ref, out_ref, acc_ref, cn_ref, m_ref, qk_ref, *, n_ctx_chunks, n_chunks):
    t, dh, nh = MLSTM_CHUNK, MLSTM_HEAD, N_MLSTM_HEADS
    acc_ref[...] = jnp.zeros_like(acc_ref)
    cn_ref[...] = jnp.zeros_like(cn_ref)
    m_ref[...] = jnp.zeros_like(m_ref)
    slab = 2 * LANES
    for half, src in enumerate((q_ref, k_ref)):
        for c0 in range(0, D_MLSTM, slab):
            dst = slice(half * D_MLSTM + c0, half * D_MLSTM + c0 + slab)
            qk_ref[:, dst] = _qk_conv_silu(src[0, :, c0:c0 + slab].astype(F32), cw_ref[:, dst], cb_ref[:, dst],
                                           cs_ref[:, dst], n_ctx_chunks * t).astype(BF16)

    ones = jnp.ones((t, dh), BF16)

    def step(j, carry):
        chains = []
        for d in range(2):
            rev = d == 1
            if rev:
                c = jnp.where(j < n_ctx_chunks, n_ctx_chunks - 1 - j, n_chunks - 1 + n_ctx_chunks - j)
            else:
                c = j
            r0 = pl.multiple_of(c * t, t)
            gates = gt_ref[0, :, pl.ds(r0, t)] + gb_ref[...]
            ig = gates[2 * nh * d:2 * nh * d + nh]
            lf = _log_sigmoid(gates[2 * nh * d + nh:2 * nh * (d + 1)])
            rows = jnp.concatenate([_dot_split(lf, cum_ref[d]), ig], axis=0)
            hi = rows.astype(BF16)
            lo = (rows - hi.astype(F32)).astype(BF16)
            rows = hi.astype(F32) + lo.astype(F32)
            f_rows, ig = rows[0:nh], rows[nh:2 * nh]
            cols = _tn_dot(hi, sel_ref[...]) + _tn_dot(lo, sel_ref[...])
            for hd in range(nh):
                chains.append(dict(d=d, rev=rev, r0=r0, idx=d * nh + hd, cs=slice(hd * dh, (hd + 1) * dh),
                                   ks=slice(D_MLSTM + hd * dh, D_MLSTM + (hd + 1) * dh),
                                   ig=ig[hd:hd + 1], f_row=f_rows[hd:hd + 1],
                                   f_col=cols[:, hd * dh:(hd + 1) * dh],
                                   i_col=cols[:, (nh + hd) * dh:(nh + hd + 1) * dh]))
        for ch in chains:
            ch["q"] = qk_ref[pl.ds(ch["r0"], t), ch["cs"]]
            ch["k"] = qk_ref[pl.ds(ch["r0"], t), ch["ks"]]
            ch["v"] = v_ref[0, pl.ds(ch["r0"], t), ch["cs"]]
            ch["qk"] = _nt_dot(ch["q"], ch["k"])
            ch["cn0"] = cn_ref[ch["idx"]]
            ch["m0"] = m_ref[ch["idx"]:ch["idx"] + 1, 0:1]
            ch["inter"] = _dot(ch["q"], ch["cn0"].astype(BF16))
        for ch in chains:
            r_row = ch["f_row"] - ch["ig"]
            f_end = ch["f_row"][:, 0:1] if ch["rev"] else ch["f_row"][:, t - 1:t]
            log_d = ch["f_col"] - r_row + neg_ref[ch["d"]]
            log_inter = ch["f_col"] + ch["m0"]
            m_t = jnp.maximum(jnp.max(log_d, axis=1, keepdims=True), log_inter)
            ch["s_qk"] = (ch["qk"] * jnp.exp(log_d - m_t)).astype(BF16)
            ch["e_inter"] = jnp.exp(log_inter - m_t)
            ch["m_t"] = m_t
            m_loc = f_end - jnp.min(r_row, axis=1, keepdims=True)
            e_end = jnp.exp(f_end - (ch["f_col"] - ch["i_col"]) - m_loc)
            ch["w"] = jnp.concatenate([e_end * ch["v"].astype(F32), e_end], axis=1).astype(BF16)
            m_new = jnp.maximum(f_end + ch["m0"], m_loc)
            ch["a"] = jnp.exp(f_end + ch["m0"] - m_new)
            ch["bb"] = jnp.exp(m_loc - m_new)
            ch["m_new"] = m_new
        for ch in chains:
            ch["intra"] = _dot(ch["s_qk"], jnp.concatenate([ch["v"], ones], axis=1))
            ch["d_cn"] = _tn_dot(ch["k"], ch["w"])
        for ch in chains:
            num = ch["intra"][:, 0:dh] + ch["e_inter"] * ch["inter"][:, 0:dh]
            den = ch["intra"][:, dh:2 * dh] + ch["e_inter"] * ch["inter"][:, dh:2 * dh]
            acc_ref[pl.ds(ch["r0"], t), ch["cs"]] += num / jnp.maximum(jnp.abs(den), jnp.exp(-ch["m_t"]))
            cn_ref[ch["idx"]] = ch["a"] * ch["cn0"] + ch["bb"] * ch["d_cn"]
            m_ref[ch["idx"]:ch["idx"] + 1, :] = jnp.broadcast_to(ch["m_new"], (1, LANES))
        return carry

    lax.fori_loop(0, n_chunks, step, 0, unroll=2 if n_chunks % 2 == 0 else 1)

    def finish(c, carry):
        r0 = pl.multiple_of(c * t, t)
        for hd in range(nh):
            cs = slice(hd * dh, (hd + 1) * dh)
            hsum = acc_ref[pl.ds(r0, t), cs]
            y = hsum * lax.rsqrt(jnp.mean(hsum * hsum, axis=-1, keepdims=True) + EPS) * g_ref[:, cs]
            out_ref[0, pl.ds(r0, t), cs] = (y * o_ref[0, pl.ds(r0, t), cs].astype(F32)).astype(BF16)
        return carry

    lax.fori_loop(0, n_chunks, finish, 0)


def _mlstm(proj, gates_t, gate_b, norm_g, conv_w, conv_b, conv_scale, col_qk, col_v, col_o, lc):
    b, lt, _ = proj.shape
    w, t = D_MLSTM, MLSTM_CHUNK
    kern = functools.partial(_mlstm_kernel, n_ctx_chunks=lc // t, n_chunks=lt // t)
    cum, neg, sel = _mlstm_masks()
    return pl.pallas_call(
        kern,
        grid=(b,),
        in_specs=[pl.BlockSpec((1, lt, w), lambda bi: (bi, 0, col_qk // w)),
                  pl.BlockSpec((1, lt, w), lambda bi: (bi, 0, col_qk // w + 1)),
                  pl.BlockSpec((1, lt, w), lambda bi: (bi, 0, col_v // w)),
                  pl.BlockSpec((1, lt, w), lambda bi: (bi, 0, col_o // w)),
                  pl.BlockSpec((1, N_GATES, lt), lambda bi: (bi, 0, 0)),
                  pl.BlockSpec((N_GATES, 1), lambda bi: (0, 0)),
                  pl.BlockSpec((1, w), lambda bi: (0, 0)),
                  pl.BlockSpec(cum.shape, lambda bi: (0, 0, 0)),
                  pl.BlockSpec(neg.shape, lambda bi: (0, 0, 0)),
                  pl.BlockSpec(sel.shape, lambda bi: (0, 0)),
                  pl.BlockSpec((QK_CONV, 2 * w), lambda bi: (0, 0)),
                  pl.BlockSpec((1, 2 * w), lambda bi: (0, 0)),
                  pl.BlockSpec((1, 2 * w), lambda bi: (0, 0))],
        out_specs=pl.BlockSpec((1, lt, w), lambda bi: (bi, 0, 0)),
        out_shape=jax.ShapeDtypeStruct((b, lt, w), BF16),
        scratch_shapes=[pltpu.VMEM((lt, w), F32),
                        pltpu.VMEM((2 * N_MLSTM_HEADS, MLSTM_HEAD, 2 * MLSTM_HEAD), F32),
                        pltpu.VMEM((2 * N_MLSTM_HEADS, LANES), F32),
                        pltpu.VMEM((lt, 2 * w), BF16)],
        compiler_params=_cparams(("parallel",)),
        name="mlstm_mixer",
    )(proj, proj, proj, proj, gates_t, gate_b, norm_g.reshape(1, w), cum, neg, sel,
      conv_w, conv_b.reshape(1, 2 * w), conv_scale)


def _attn_kernel(sc_ref, q_ref, k_ref, v_ref, lam_ref, g_ref, o_ref, vext_ref, *, lc, ctx_tiles):
    i = pl.program_id(2)
    dv = DIFF_V_HEAD
    lp = lam_ref[...]
    lam = (jnp.exp(jnp.sum(lp[0:1] * lp[1:2], axis=1, keepdims=True))
           - jnp.exp(jnp.sum(lp[2:3] * lp[3:4], axis=1, keepdims=True)) + sc_ref[0])
    lane = lax.broadcasted_iota(jnp.int32, (1, dv), 1)

    n_heads = q_ref.shape[2] // dv

    @pl.when(i == 0)
    def _():
        for h in range(n_heads):
            vext_ref[:, 2 * h * dv:(2 * h + 1) * dv] = v_ref[0, :, h * dv:(h + 1) * dv]
            vext_ref[:, (2 * h + 1) * dv:(2 * h + 2) * dv] = jnp.ones((v_ref.shape[1], dv), BF16)

    def attend(nk, kc):
        for h in range(n_heads):
            hs = slice(h * dv, (h + 1) * dv)
            q = q_ref[0, :, hs]
            outs = []
            for c in range(2):
                keep = (lane < DIFF_QK_HEAD) if c == 0 else (lane >= DIFF_QK_HEAD)
                qc = jnp.where(keep, q, jnp.zeros_like(q))
                m_run, acc = None, None
                for k0 in range(0, nk, kc):
                    s = _nt_dot(qc, k_ref[0, k0:k0 + kc, hs])
                    m_j = jnp.max(s, axis=1, keepdims=True)
                    m_new = m_j if m_run is None else jnp.maximum(m_run, m_j)
                    pv = _dot(jnp.exp(s - m_new).astype(BF16), vext_ref[k0:k0 + kc, 2 * h * dv:(2 * h + 2) * dv])
                    acc = pv if m_run is None else acc * jnp.exp(m_run - m_new) + pv
                    m_run = m_new
                outs.append(acc[:, 0:dv] / acc[:, dv:2 * dv])
            o = outs[0] - lam * outs[1]
            y = o * lax.rsqrt(jnp.mean(o * o, axis=-1, keepdims=True) + EPS) * g_ref[...]
            o_ref[0, :, hs] = (y * sc_ref[1]).astype(BF16)

    @pl.when(i < ctx_tiles)
    def _():
        attend(lc, lc)

    @pl.when(i >= ctx_tiles)
    def _():
        nk = k_ref.shape[1]
        attend(nk, ATTN_KEY_CHUNK if nk % ATTN_KEY_CHUNK == 0 else nk)


def _attention(scal, proj, lam_p, subln_g, col_q, col_k, col_v, lc):
    b, lt, _ = proj.shape
    w, dv = D_DIFF, DIFF_V_HEAD
    kern = functools.partial(_attn_kernel, lc=lc, ctx_tiles=lc // ROW_TILE)
    return pl.pallas_call(
        kern,
        grid=(b, 1, lt // ROW_TILE),
        in_specs=[pl.BlockSpec(memory_space=pltpu.SMEM),
                  pl.BlockSpec((1, ROW_TILE, w), lambda bi, hi, i: (bi, i, col_q // w)),
                  pl.BlockSpec((1, lt, w), lambda bi, hi, i: (bi, 0, col_k // w)),
                  pl.BlockSpec((1, lt, w), lambda bi, hi, i: (bi, 0, col_v // w)),
                  pl.BlockSpec((4, DIFF_QK_HEAD), lambda bi, hi, i: (0, 0)),
                  pl.BlockSpec((1, dv), lambda bi, hi, i: (0, 0))],
        out_specs=pl.BlockSpec((1, ROW_TILE, w), lambda bi, hi, i: (bi, i, 0)),
        out_shape=jax.ShapeDtypeStruct((b, lt, w), BF16),
        scratch_shapes=[pltpu.VMEM((lt, 2 * w), BF16)],
        compiler_params=_cparams(("parallel", "parallel", "arbitrary")),
        name="diff_attention",
    )(scal, proj, proj, proj, lam_p, subln_g.reshape(1, dv))


def _merge_kernel(x_ref, mod_ref, y5_ref, sz_ref, hb_ref, mz_ref, oc_ref, dz_ref, gl_ref,
                  gw_ref, gb_ref, wa_ref, wb_ref, wc_ref, wo_ref, o_ref, *, d_model):
    f = lambda r: r[0].astype(F32)
    rows = x_ref.shape[1]
    y = jnp.concatenate([y5_ref[j, :, 0].reshape(rows, LANES) for j in range(S5_SLABS)], axis=1).astype(F32)
    ge = 0.5 * y * (1.0 + jnp.tanh(math.sqrt(2.0 / math.pi) * (y + 0.044715 * (y * y * y))))
    t = _dot(ge.astype(BF16), gw_ref[...]) + gb_ref[...]
    ya = t[:, 0:D_SSM] * jax.nn.sigmoid(t[:, D_SSM:2 * D_SSM])
    pa = _dot((ya * f(sz_ref)).astype(BF16), wa_ref[...])
    pb = _dot((f(hb_ref) * f(mz_ref)).astype(BF16), wb_ref[...])
    pc = _dot((f(oc_ref) * f(dz_ref)).astype(BF16), wc_ref[...])
    gates = f(gl_ref)
    m = (gates[:, 0:d_model] * pa + gates[:, d_model:2 * d_model] * pb
         + gates[:, 2 * d_model:3 * d_model] * pc)
    out = _dot(m.astype(BF16), wo_ref[...])
    o_ref[0] = x_ref[0] + mod_ref[0, :, 2 * d_model:3 * d_model] * out


def _merge(x_all, mods3, y5, hb, oc, proj, glu_w, glu_b, wa, wb, wc, wo, cols, n_batch, lc, skip_ctx):
    b, lt, d = x_all.shape
    ctx_tiles = lc // ROW_TILE
    t0 = ctx_tiles if skip_ctx else 0
    w = D_SSM
    blk = lambda name: cols[name] // w
    blk_rows = ROW_TILE // S5_CHUNK
    tok = lambda j: pl.BlockSpec((1, ROW_TILE, w), lambda bi, i: (bi, i + t0, j))
    full = lambda a: pl.BlockSpec(a.shape, lambda bi, i: (0,) * a.ndim)
    kern = functools.partial(_merge_kernel, d_model=d)
    return pl.pallas_call(
        kern,
        grid=(b, lt // ROW_TILE - t0),
        in_specs=[pl.BlockSpec((1, ROW_TILE, d), lambda bi, i: (bi, i + t0, 0)),
                  pl.BlockSpec((1, 1, 3 * d), lambda bi, i: (jnp.where(i + t0 < ctx_tiles, n_batch, bi), 0, 0)),
                  pl.BlockSpec((S5_SLABS, blk_rows, 1, S5_CHUNK, LANES), lambda bi, i: (0, i + t0, bi, 0, 0)),
                  tok(blk("sz")), tok(0), tok(blk("mz")), tok(0), tok(blk("dz")),
                  pl.BlockSpec((1, ROW_TILE, N_BRANCH * d), lambda bi, i: (bi, i + t0, 0)),
                  full(glu_w), full(glu_b), full(wa), full(wb), full(wc), full(wo)],
        out_specs=pl.BlockSpec((1, ROW_TILE, d), lambda bi, i: (bi, i, 0)),
        out_shape=jax.ShapeDtypeStruct((b, lt - t0 * ROW_TILE, d), F32),
        compiler_params=_cparams(("parallel", "parallel")),
        name="merge_outproj",
    )(x_all, mods3, y5, proj, hb, proj, oc, proj, proj, glu_w, glu_b, wa, wb, wc, wo)


def _rope_tables(lc, l):
    half = DIFF_QK_HEAD // 2
    rows = l // GRID_W
    row = jnp.repeat(jnp.arange(rows), GRID_W).astype(F32)
    col = jnp.tile(jnp.arange(GRID_W), rows).astype(F32)
    inv = jnp.power(ROPE_BASE, -jnp.arange(0, half, 2, dtype=F32) / half)
    ang_r, ang_c = row[:, None] * inv, col[:, None] * inv
    cos64 = jnp.concatenate([jnp.cos(ang_r), jnp.cos(ang_r), jnp.cos(ang_c), jnp.cos(ang_c)], axis=1)
    sin64 = jnp.concatenate([-jnp.sin(ang_r), jnp.sin(ang_r), -jnp.sin(ang_c), jnp.sin(ang_c)], axis=1)
    cos_t = jnp.concatenate([jnp.ones((lc, DIFF_QK_HEAD), F32), cos64], axis=0)
    sin_t = jnp.concatenate([jnp.zeros((lc, DIFF_QK_HEAD), F32), sin64], axis=0)
    return jnp.tile(cos_t, (1, 2)), jnp.tile(sin_t, (1, 2))


def kernel(x, c, ctx, c_ctx, norm_g, ada_w, ada_b, w_in, ssm_lam_re, ssm_lam_im, ssm_log_step, ssm_b_re, ssm_b_im, ssm_c_re, ssm_c_im, ssm_d, ssm_glu_w, ssm_glu_b, w_ssm_out, ml_conv_w, ml_conv_b, ml_gate_b, ml_norm_g, w_ml_out, da_qnorm_g, da_knorm_g, da_lambda, da_subln_g, w_da_out, w_out):
    n_batch, l, d = x.shape
    lc = ctx.shape[1]
    lt = lc + l
    depth = w_in.shape[0]
    cols = _cols(d)
    t5 = S5_CHUNK
    n_rows = (lt // t5) * n_batch
    assert lc % ROW_TILE == 0 and l % ROW_TILE == 0 and lc % MLSTM_CHUNK == 0 and n_batch % 8 == 0
    assert MLSTM_CHUNK == MLSTM_HEAD == LANES

    x_all = jnp.concatenate([ctx, x], axis=1)

    mod_rows = -(-(n_batch + 1) // 8) * 8
    cvec = jnp.zeros((mod_rows, d), F32).at[:n_batch].set(c).at[n_batch].set(c_ctx)
    mods = _modulation(cvec, ada_w, ada_b)

    g0 = 2 * D_SSM + 5 * D_MLSTM
    g1 = g0 + N_GATES
    m0 = g1 + 4 * D_DIFF
    w_p = jnp.concatenate([w_in[:, :, m0:], w_in[:, :, D_SSM:g0], w_in[:, :, g1:m0]], axis=2).astype(BF16)
    w_su = w_in[:, :, :D_SSM].astype(BF16)
    w_gt = jnp.swapaxes(w_in[:, :, g0:g1], 1, 2).astype(BF16)

    cos_t, sin_t = _rope_tables(lc, l)
    ones_bd = jnp.kron(jnp.eye(2 * LANES // DIFF_QK_HEAD, dtype=F32),
                       jnp.ones((DIFF_QK_HEAD, DIFF_QK_HEAD), F32)).astype(BF16)
    qk_scale = jnp.concatenate([jnp.ones((1, D_MLSTM), F32), jnp.full((1, D_MLSTM), MLSTM_HEAD ** -0.5, F32)], 1)
    perm_in, perm_out = _s5_lane_permutations()
    s5_m1, s5_m2, s5_dec = _s5_block_operators(ssm_lam_re, ssm_lam_im, ssm_log_step, ssm_b_re, ssm_b_im,
                                               ssm_c_re, ssm_c_im, ssm_d)

    for li in range(depth):
        last = li == depth - 1
        mods3 = mods[li].reshape(mod_rows, 1, 3 * d)
        qg = jnp.tile(da_qnorm_g[li].astype(F32), D_DIFF // DIFF_QK_HEAD).reshape(1, D_DIFF)
        kg = jnp.tile(da_knorm_g[li].astype(F32), D_DIFF // DIFF_QK_HEAD).reshape(1, D_DIFF)
        proj, su, gates_t = _inproj(x_all, mods3, norm_g[li], w_p[li], w_su[li], w_gt[li], cos_t, sin_t, qg, kg,
                                    ones_bd, cols, n_batch, lc)

        xb = _s5_gather(su.reshape(S5_SLABS, n_rows, S5_WIDE), perm_in)
        yb = _s5_mixer(xb, s5_m1[li], s5_m2[li], s5_dec[li], n_batch, lc // t5)
        y5 = _s5_scatter(yb, perm_out).reshape(su.shape)

        hb = _mlstm(proj, gates_t, ml_gate_b[li].astype(F32).reshape(N_GATES, 1), ml_norm_g[li],
                    ml_conv_w[li], ml_conv_b[li], qk_scale, cols["mqk"], cols["mv"], cols["mo"], lc)

        lam_init = 0.8 - 0.6 * math.exp(-0.3 * li)
        scal = jnp.array([lam_init, 1.0 - lam_init], F32)
        oc = _attention(scal, proj, da_lambda[li].astype(F32), da_subln_g[li].astype(F32),
                        cols["dq"], cols["dk"], cols["dv"], lc)

        x_all = _merge(x_all, mods3, y5, hb, oc, proj, ssm_glu_w[li].astype(BF16),
                       ssm_glu_b[li].astype(F32).reshape(1, -1), w_ssm_out[li].astype(BF16),
                       w_ml_out[li].astype(BF16), w_da_out[li].astype(BF16), w_out[li].astype(BF16),
                       cols, n_batch, lc, skip_ctx=last)
    return x_all
```

```python
import functools
import math

import jax
import jax.numpy as jnp
from jax import lax
from jax.experimental import pallas as pl
from jax.experimental.pallas import tpu as pltpu

F32 = jnp.float32
BF16 = jnp.bfloat16
HIGHEST = lax.Precision.HIGHEST

EPS = 1e-6
GRID_W = 64
ROPE_BASE = 10000.0
LANES = 128
SUBLANES = 8

D_SSM = 512
SSM_GROUP = 16
N_SSM_GROUPS = D_SSM // SSM_GROUP
SSM_STATE = 64
S5_CHUNK = 16
S5_K = S5_CHUNK * SSM_GROUP
S5_LANE_GROUPS = LANES // SSM_GROUP
S5_SLABS = D_SSM // LANES
S5_WIDE = S5_CHUNK * LANES
S5_ROW_TILES = (768, 512, 256)
D_MLSTM = 512
N_MLSTM_HEADS = 4
MLSTM_HEAD = D_MLSTM // N_MLSTM_HEADS
MLSTM_CHUNK = 128
QK_CONV = 3
N_GATES = 4 * N_MLSTM_HEADS
N_DIFF_HEADS = 4
DIFF_QK_HEAD = 64
DIFF_V_HEAD = 2 * DIFF_QK_HEAD
D_DIFF = N_DIFF_HEADS * DIFF_V_HEAD
ROPE_PAIR = DIFF_QK_HEAD // 4
ATTN_KEY_CHUNK = 768
N_BRANCH = 3

ROW_TILE = 256
VMEM_LIMIT = 56 * 1024 * 1024


def _cols(d_model):
    off = {"gl": 0}
    o = N_BRANCH * d_model
    for name, w in (("sz", D_SSM), ("mqk", 2 * D_MLSTM), ("mv", D_MLSTM), ("mo", D_MLSTM),
                    ("mz", D_MLSTM), ("dq", D_DIFF), ("dk", D_DIFF), ("dv", D_DIFF), ("dz", D_DIFF)):
        off[name] = o
        o += w
    off["total"] = o
    return off


def _cparams(sem):
    return pltpu.CompilerParams(dimension_semantics=sem, vmem_limit_bytes=VMEM_LIMIT)


def _nt_dot(a, b):
    return lax.dot_general(a, b, (((1,), (1,)), ((), ())), preferred_element_type=F32)


def _tn_dot(a, b):
    return lax.dot_general(a, b, (((0,), (0,)), ((), ())), preferred_element_type=F32)


def _dot(a, b):
    return jnp.dot(a, b, preferred_element_type=F32)


def _dot_split(x, w):
    hi = x.astype(BF16)
    lo = (x - hi.astype(F32)).astype(BF16)
    return _dot(hi, w) + _dot(lo, w)


def _mods_kernel(c_ref, w_ref, b_ref, o_ref):
    c = c_ref[...]
    act = c * jax.nn.sigmoid(c)
    o_ref[0] = jnp.dot(act, w_ref[0], preferred_element_type=F32, precision=HIGHEST) + b_ref[0]


def _modulation(cvec, ada_w, ada_b):
    depth, d, d3 = ada_w.shape
    rows = cvec.shape[0]
    return pl.pallas_call(
        _mods_kernel,
        grid=(depth, d3 // d),
        in_specs=[pl.BlockSpec((rows, d), lambda l, j: (0, 0)),
                  pl.BlockSpec((1, d, d), lambda l, j: (l, 0, j)),
                  pl.BlockSpec((1, 1, d), lambda l, j: (l, 0, j))],
        out_specs=pl.BlockSpec((1, rows, d), lambda l, j: (l, 0, j)),
        out_shape=jax.ShapeDtypeStruct((depth, rows, d3), F32),
        compiler_params=_cparams(("arbitrary", "arbitrary")),
        name="adaln_modulation",
    )(cvec, ada_w, ada_b.reshape(depth, 1, d3))


def _qk_norm_rope(x, gain, cos, sin, ones_bd, scale):
    half = ones_bd.shape[0]
    ms = jnp.concatenate([_dot_split(x[:, c0:c0 + half] * x[:, c0:c0 + half], ones_bd)
                          for c0 in range(0, x.shape[1], half)], axis=1) * (1.0 / DIFF_QK_HEAD)
    y = x * lax.rsqrt(ms + EPS) * gain
    lane = lax.broadcasted_iota(jnp.int32, (1, LANES), 1)
    first_half = (lane // ROPE_PAIR) % 2 == 0
    outs = []
    for c0 in range(0, y.shape[1], LANES):
        yb = y[:, c0:c0 + LANES]
        partner = jnp.where(first_half, pltpu.roll(yb, LANES - ROPE_PAIR, axis=1), pltpu.roll(yb, ROPE_PAIR, axis=1))
        outs.append(yb * cos + partner * sin)
    return jnp.concatenate(outs, axis=1) * scale


def _inproj_kernel(x_ref, mod_ref, g_ref, w_ref, wsu_ref, wgt_ref, cos_ref, sin_ref, qg_ref, kg_ref, ones_ref,
                   o_ref, su_ref, ogt_ref, *, d_model, col_chunk, col_q, col_k, sigmoid_cols, silu_cols):
    x = x_ref[0]
    ms = jnp.mean(x * x, axis=-1, keepdims=True)
    y = x * lax.rsqrt(ms + EPS) * g_ref[...]
    sh = mod_ref[0, :, 0:d_model]
    sc = mod_ref[0, :, d_model:2 * d_model]
    hb = (y * (1.0 + sc) + sh).astype(BF16)
    n_out = o_ref.shape[-1]
    for c0 in range(0, n_out, col_chunk):
        acc = _dot(hb, w_ref[:, c0:c0 + col_chunk])
        if c0 == col_q:
            acc = _qk_norm_rope(acc, qg_ref[...], cos_ref[...], sin_ref[...], ones_ref[...], DIFF_QK_HEAD ** -0.5)
        elif c0 == col_k:
            acc = _qk_norm_rope(acc, kg_ref[...], cos_ref[...], sin_ref[...], ones_ref[...], 1.0)
        elif c0 in sigmoid_cols:
            acc = jax.nn.sigmoid(acc)
        elif c0 in silu_cols:
            acc = acc * jax.nn.sigmoid(acc)
        o_ref[0, :, c0:c0 + col_chunk] = acc.astype(BF16)
    rows = x.shape[0]
    for j in range(0, S5_SLABS, 2):
        su = _dot(hb, wsu_ref[:, j * LANES:(j + 2) * LANES]).astype(BF16)
        for jj in range(2):
            su_ref[j + jj, :, 0] = su[:, jj * LANES:(jj + 1) * LANES].reshape(rows // S5_CHUNK, S5_CHUNK, LANES)
    ogt_ref[0] = _nt_dot(wgt_ref[...], hb)


def _inproj(x_all, mods3, norm_g, w_p, w_su, w_gt, cos_t, sin_t, qg, kg, ones_bd, cols, n_batch, lc):
    b, lt, d = x_all.shape
    n_out = w_p.shape[1]
    ctx_tiles = lc // ROW_TILE
    chunk = D_DIFF
    assert all(cols[k] % chunk == 0 for k in ("dq", "dk", "sz", "mo", "mz", "dz")) and cols["sz"] == N_BRANCH * d
    sigmoid_cols = tuple(range(cols["gl"], cols["sz"], chunk)) + (cols["mo"],)
    silu_cols = (cols["sz"], cols["mz"], cols["dz"])
    kern = functools.partial(_inproj_kernel, d_model=d, col_chunk=chunk, col_q=cols["dq"], col_k=cols["dk"],
                             sigmoid_cols=sigmoid_cols, silu_cols=silu_cols)
    blk_rows = ROW_TILE // S5_CHUNK
    return pl.pallas_call(
        kern,
        grid=(b, lt // ROW_TILE),
        in_specs=[pl.BlockSpec((1, ROW_TILE, d), lambda bi, i: (bi, i, 0)),
                  pl.BlockSpec((1, 1, 3 * d), lambda bi, i: (jnp.where(i < ctx_tiles, n_batch, bi), 0, 0)),
                  pl.BlockSpec((1, d), lambda bi, i: (0, 0)),
                  pl.BlockSpec((d, n_out), lambda bi, i: (0, 0), pipeline_mode=pl.Buffered(1)),
                  pl.BlockSpec((d, D_SSM), lambda bi, i: (0, 0)),
                  pl.BlockSpec((N_GATES, d), lambda bi, i: (0, 0)),
                  pl.BlockSpec((ROW_TILE, LANES), lambda bi, i: (i, 0)),
                  pl.BlockSpec((ROW_TILE, LANES), lambda bi, i: (i, 0)),
                  pl.BlockSpec((1, D_DIFF), lambda bi, i: (0, 0)),
                  pl.BlockSpec((1, D_DIFF), lambda bi, i: (0, 0)),
                  pl.BlockSpec(ones_bd.shape, lambda bi, i: (0, 0))],
        out_specs=[pl.BlockSpec((1, ROW_TILE, n_out), lambda bi, i: (bi, i, 0)),
                   pl.BlockSpec((S5_SLABS, blk_rows, 1, S5_CHUNK, LANES), lambda bi, i: (0, i, bi, 0, 0)),
                   pl.BlockSpec((1, N_GATES, ROW_TILE), lambda bi, i: (bi, 0, i))],
        out_shape=[jax.ShapeDtypeStruct((b, lt, n_out), BF16),
                   jax.ShapeDtypeStruct((S5_SLABS, lt // S5_CHUNK, b, S5_CHUNK, LANES), BF16),
                   jax.ShapeDtypeStruct((b, N_GATES, lt), F32)],
        compiler_params=_cparams(("parallel", "parallel")),
        name="adaln_inproj",
    )(x_all, mods3, norm_g.reshape(1, d), w_p, w_su, w_gt, cos_t, sin_t, qg, kg, ones_bd)


def _s5_ops_kernel(lamc_ref, lamr_ref, step_ref, btre_ref, btim_ref, ctre_ref, ctim_ref, drep_ref,
                   m1_ref, m2_ref, dec_ref):
    t, n, p, k = S5_CHUNK, SSM_GROUP, SSM_STATE, S5_K
    hp = HIGHEST
    lane = lax.broadcasted_iota(jnp.int32, (1, k), 1)
    tok = lane % t
    tokf = tok.astype(F32)
    expand = (lax.broadcasted_iota(jnp.int32, (n, k), 0) == lane // t).astype(F32)
    pow_rows = -(-(t + 1) // SUBLANES) * SUBLANES
    pow_idx = lax.broadcasted_iota(jnp.int32, (pow_rows, 1), 0).astype(F32)
    tin_rows = [None] * t
    states, readouts, decs = [], [], []
    for d in range(2):
        dt = jnp.exp(step_ref[0, 0, :, d:d + 1])
        lre_c = jnp.minimum(lamc_ref[0, 0, :, 2 * d:2 * d + 1], -1e-4)
        lim_c = lamc_ref[0, 0, :, 2 * d + 1:2 * d + 2]
        lre_r = jnp.minimum(lamr_ref[0, 0, 2 * d:2 * d + 1, :], -1e-4)
        lim_r = lamr_ref[0, 0, 2 * d + 1:2 * d + 2, :]
        ldre_c, ldim_c = lre_c * dt, lim_c * dt
        ldre_r, ldim_r = lre_r * dt, lim_r * dt
        mag = jnp.exp(ldre_r)
        nre, nim = mag * jnp.cos(ldim_r) - 1.0, mag * jnp.sin(ldim_r)
        den = lre_r * lre_r + lim_r * lim_r
        coef_re = (nre * lre_r + nim * lim_r) / den
        coef_im = (nim * lre_r - nre * lim_r) / den
        btre, btim = btre_ref[0, d, 0], btim_ref[0, d, 0]
        bb_re = coef_re * btre - coef_im * btim
        bb_im = coef_re * btim + coef_im * btre
        ct_re = jnp.dot(ctre_ref[0, d, 0], expand, preferred_element_type=F32, precision=hp)
        ct_im = jnp.dot(ctim_ref[0, d, 0], expand, preferred_element_type=F32, precision=hp)

        def c_times_power(kvec):
            pm = jnp.exp(kvec * ldre_c)
            pr, pi = pm * jnp.cos(kvec * ldim_c), pm * jnp.sin(kvec * ldim_c)
            return ct_re * pr - ct_im * pi, ct_re * pi + ct_im * pr

        q_re, q_im = c_times_power(tokf if d == 0 else (t - 1.0) - tokf)
        v = (jnp.dot(bb_re, q_re, preferred_element_type=F32, precision=hp)
             - jnp.dot(bb_im, q_im, preferred_element_type=F32, precision=hp))
        for a in range(t):
            shift = a if d == 0 else (a + k - (t - 1)) % k
            blk = pltpu.roll(v, shift, axis=1) if shift else v
            blk = jnp.where((tok >= a) if d == 0 else (tok <= a), blk, 0.0)
            tin_rows[a] = blk if tin_rows[a] is None else tin_rows[a] + blk
        pmag = jnp.exp(pow_idx * ldre_r)
        pw_re, pw_im = pmag * jnp.cos(pow_idx * ldim_r), pmag * jnp.sin(pow_idx * ldim_r)
        ri, ir = [], []
        for a in range(t):
            kx = t - 1 - a if d == 0 else a
            pr, pi = pw_re[kx:kx + 1], pw_im[kx:kx + 1]
            s_re, s_im = bb_re * pr - bb_im * pi, bb_re * pi + bb_im * pr
            ri.append(jnp.concatenate([s_re, s_im], axis=1))
            ir.append(jnp.concatenate([s_im, s_re], axis=1))
        states += [jnp.concatenate(ri, axis=0), jnp.concatenate(ir, axis=0)]
        ca_re, ca_im = c_times_power(tokf + 1.0 if d == 0 else t - tokf)
        readouts += [ca_re, -ca_im]
        decs += [jnp.concatenate([pw_re[t:t + 1], pw_re[t:t + 1]], axis=1),
                 jnp.concatenate([-pw_im[t:t + 1], pw_im[t:t + 1]], axis=1)]
    row = lax.broadcasted_iota(jnp.int32, (k, 1), 0)
    skip = jnp.where(lane == (row % n) * t + row // n, drep_ref[0, 0], 0.0)
    tin = jnp.concatenate(tin_rows, axis=0) + skip
    m1_ref[0, 0] = jnp.concatenate([tin] + states, axis=1).astype(BF16)
    m2_ref[0, 0] = jnp.concatenate(readouts, axis=0).astype(BF16)
    dec_ref[0, 0] = jnp.concatenate(decs, axis=0)


def _s5_block_operators(lam_re, lam_im, log_step, b_re, b_im, c_re, c_im, d_skip):
    depth = lam_re.shape[0]
    g, p, n, k = N_SSM_GROUPS, SSM_STATE, SSM_GROUP, S5_K
    f = lambda a: a.astype(F32)
    lam = jnp.stack([f(lam_re[:, 0]), f(lam_im[:, 0]), f(lam_re[:, 1]), f(lam_im[:, 1])], axis=-1)
    step = jnp.swapaxes(f(log_step), 1, 2).reshape(depth, g, 1, 2)
    d_rep = jnp.tile(f(d_skip).reshape(depth, g, 1, n), (1, 1, S5_CHUNK, 1)).reshape(depth, g, k, 1)
    per = lambda shape: pl.BlockSpec((1, 1) + shape, lambda l, i: (l, i) + (0,) * len(shape))
    both = lambda shape: pl.BlockSpec((1, 2, 1) + shape, lambda l, i: (l, 0, i) + (0,) * len(shape))
    sw = 2 * p
    return pl.pallas_call(
        _s5_ops_kernel,
        grid=(depth, g),
        in_specs=[per((p, 4)), per((4, p)), per((1, 2)), both((n, p)), both((n, p)), both((p, n)), both((p, n)),
                  per((k, 1))],
        out_specs=[per((k, k + 4 * sw)), per((2 * sw, k)), per((4, sw))],
        out_shape=[jax.ShapeDtypeStruct((depth, g, k, k + 4 * sw), BF16),
                   jax.ShapeDtypeStruct((depth, g, 2 * sw, k), BF16),
                   jax.ShapeDtypeStruct((depth, g, 4, sw), F32)],
        compiler_params=_cparams(("parallel", "parallel")),
        name="s5_block_operators",
    )(lam, jnp.swapaxes(lam, 2, 3), step, jnp.swapaxes(f(b_re), 3, 4), jnp.swapaxes(f(b_im), 3, 4),
      jnp.swapaxes(f(c_re), 3, 4), jnp.swapaxes(f(c_im), 3, 4), d_rep)


def _s5_lane_permutations():
    src = jnp.arange(S5_WIDE)
    s, rem = src // LANES, src % LANES
    g8, n = rem // SSM_GROUP, rem % SSM_GROUP
    cols = jnp.arange(S5_WIDE)[None, :]
    p_in = ((g8 * S5_K + s * SSM_GROUP + n)[:, None] == cols).astype(BF16)
    p_out = ((g8 * S5_K + n * S5_CHUNK + s)[:, None] == cols).astype(BF16)
    return p_in, p_out


def _s5_gather_kernel(l_ref, p_ref, o_ref):
    res = _dot(l_ref[0], p_ref[...])
    for g8 in range(S5_LANE_GROUPS):
        o_ref[g8] = res[:, g8 * S5_K:(g8 + 1) * S5_K].astype(BF16)


def _s5_row_tile(n_rows):
    return next(t for t in S5_ROW_TILES if n_rows % t == 0)


def _s5_gather(su_wide, perm):
    slabs, r, wide = su_wide.shape
    rt = _s5_row_tile(r)
    return pl.pallas_call(
        _s5_gather_kernel,
        grid=(slabs, r // rt),
        in_specs=[pl.BlockSpec((1, rt, wide), lambda j, i: (j, i, 0)),
                  pl.BlockSpec((wide, wide), lambda j, i: (0, 0), pipeline_mode=pl.Buffered(1))],
        out_specs=pl.BlockSpec((S5_LANE_GROUPS, rt, S5_K), lambda j, i: (j, i, 0)),
        out_shape=jax.ShapeDtypeStruct((slabs * S5_LANE_GROUPS, r, S5_K), BF16),
        compiler_params=_cparams(("parallel", "parallel")),
        name="s5_gather",
    )(su_wide, perm)


def _s5_scatter_kernel(y_ref, p_ref, o_ref):
    y = jnp.concatenate([y_ref[g8] for g8 in range(S5_LANE_GROUPS)], axis=1)
    o_ref[0] = _nt_dot(y, p_ref[...]).astype(BF16)


def _s5_scatter(y_blocks, perm):
    g, r, k = y_blocks.shape
    slabs = g // S5_LANE_GROUPS
    rt = _s5_row_tile(r)
    return pl.pallas_call(
        _s5_scatter_kernel,
        grid=(slabs, r // rt),
        in_specs=[pl.BlockSpec((S5_LANE_GROUPS, rt, k), lambda j, i: (j, i, 0)),
                  pl.BlockSpec((S5_WIDE, S5_WIDE), lambda j, i: (0, 0), pipeline_mode=pl.Buffered(1))],
        out_specs=pl.BlockSpec((1, rt, S5_WIDE), lambda j, i: (j, i, 0)),
        out_shape=jax.ShapeDtypeStruct((slabs, r, S5_WIDE), BF16),
        compiler_params=_cparams(("parallel", "parallel")),
        name="s5_scatter",
    )(y_blocks, perm)


def _s5_kernel(x_ref, m1_ref, m2_ref, dec_ref, o_ref, p_ref, h_ref, *, n_batch, n_ctx_chunks, n_chunks):
    k, sw = S5_K, 2 * SSM_STATE
    p_ref[...] = _dot(x_ref[0], m1_ref[0])
    a1f, a2f = dec_ref[0, 0:1, :], dec_ref[0, 1:2, :]
    a1b, a2b = dec_ref[0, 2:3, :], dec_ref[0, 3:4, :]
    zero = jnp.zeros((n_batch, sw), F32)

    def step(j, carry):
        hf, hfs, hb, hbs = carry
        cb = jnp.where(j < n_ctx_chunks, n_ctx_chunks - 1 - j, n_chunks - 1 + n_ctx_chunks - j)
        rf = pl.multiple_of(j * n_batch, n_batch)
        rb = pl.multiple_of(cb * n_batch, n_batch)
        h_ref[pl.ds(rf, n_batch), 0:sw] = hf
        h_ref[pl.ds(rb, n_batch), sw:2 * sw] = hb
        sf = p_ref[pl.ds(rf, n_batch), k:k + sw]
        sfs = p_ref[pl.ds(rf, n_batch), k + sw:k + 2 * sw]
        sb = p_ref[pl.ds(rb, n_batch), k + 2 * sw:k + 3 * sw]
        sbs = p_ref[pl.ds(rb, n_batch), k + 3 * sw:k + 4 * sw]
        return (a1f * hf + a2f * hfs + sf, a1f * hfs - a2f * hf + sfs,
                a1b * hb + a2b * hbs + sb, a1b * hbs - a2b * hb + sbs)

    lax.fori_loop(0, n_chunks, step, (zero, zero, zero, zero))
    inter = _dot(h_ref[...].astype(BF16), m2_ref[0])
    o_ref[0] = (p_ref[:, 0:k] + inter).astype(BF16)


def _s5_mixer(x_blocks, m1, m2, dec, n_batch, n_ctx_chunks):
    g, r, k = x_blocks.shape
    n_chunks = r // n_batch
    kern = functools.partial(_s5_kernel, n_batch=n_batch, n_ctx_chunks=n_ctx_chunks, n_chunks=n_chunks)
    return pl.pallas_call(
        kern,
        grid=(g,),
        in_specs=[pl.BlockSpec((1, r, k), lambda i: (i, 0, 0)),
                  pl.BlockSpec((1, k, m1.shape[2]), lambda i: (i, 0, 0)),
                  pl.BlockSpec((1, m2.shape[1], k), lambda i: (i, 0, 0)),
                  pl.BlockSpec((1, 4, 2 * SSM_STATE), lambda i: (i, 0, 0))],
        out_specs=pl.BlockSpec((1, r, k), lambda i: (i, 0, 0)),
        out_shape=jax.ShapeDtypeStruct((g, r, k), BF16),
        scratch_shapes=[pltpu.VMEM((r, m1.shape[2]), F32), pltpu.VMEM((r, 4 * SSM_STATE), F32)],
        compiler_params=_cparams(("parallel",)),
        name="s5_mixer",
    )(x_blocks, m1, m2, dec)


def _qk_conv_silu(x, w, b, scale, lc):
    lt = x.shape[0]
    row = lax.broadcasted_iota(jnp.int32, (lt, 1), 0)
    prev = jnp.where((row == 0) | (row == lc), 0.0, pltpu.roll(x, 1, axis=0))
    nxt = jnp.where((row == lc - 1) | (row == lt - 1), 0.0, pltpu.roll(x, lt - 1, axis=0))
    y = w[0:1, :] * prev + w[1:2, :] * x + w[2:3, :] * nxt + b
    return y * jax.nn.sigmoid(y) * scale


def _log_sigmoid(x):
    return jnp.minimum(x, 0.0) - jnp.log1p(jnp.exp(-jnp.abs(x)))


def _mlstm_masks():
    t, nv = MLSTM_CHUNK, 2 * N_MLSTM_HEADS
    r_i = jnp.arange(t)[:, None]
    c_i = jnp.arange(t)[None, :]
    cum = jnp.stack([r_i <= c_i, r_i >= c_i]).astype(BF16)
    neg = jnp.where(jnp.stack([c_i <= r_i, c_i >= r_i]), 0.0, -jnp.inf).astype(F32)
    sel = (jnp.arange(nv * LANES)[None, :] // LANES == jnp.arange(nv)[:, None]).astype(BF16)
    return cum, neg, sel


def _mlstm_kernel(q_ref, k_ref, v_ref, o_ref, gt_ref, gb_ref, g_ref, cum_ref, neg_ref, sel_ref,
                  cw_ref, cb_ref, cs_ref, out_ref, acc_ref, cn_ref, m_ref, qk_ref, *, n_ctx_chunks, n_chunks):
    t, dh, nh = MLSTM_CHUNK, MLSTM_HEAD, N_MLSTM_HEADS
    acc_ref[...] = jnp.zeros_like(acc_ref)
    cn_ref[...] = jnp.zeros_like(cn_ref)
    m_ref[...] = jnp.zeros_like(m_ref)
    slab = 2 * LANES
    for half, src in enumerate((q_ref, k_ref)):
        for c0 in range(0, D_MLSTM, slab):
            dst = slice(half * D_MLSTM + c0, half * D_MLSTM + c0 + slab)
            qk_ref[:, dst] = _qk_conv_silu(src[0, :, c0:c0 + slab].astype(F32), cw_ref[:, dst], cb_ref[:, dst],
                                           cs_ref[:, dst], n_ctx_chunks * t).astype(BF16)

    ones = jnp.ones((t, dh), BF16)

    def step(j, carry):
        chains = []
        for d in range(2):
            rev = d == 1
            if rev:
                c = jnp.where(j < n_ctx_chunks, n_ctx_chunks - 1 - j, n_chunks - 1 + n_ctx_chunks - j)
            else:
                c = j
            r0 = pl.multiple_of(c * t, t)
            gates = gt_ref[0, :, pl.ds(r0, t)] + gb_ref[...]
            ig = gates[2 * nh * d:2 * nh * d + nh]
            lf = _log_sigmoid(gates[2 * nh * d + nh:2 * nh * (d + 1)])
            rows = jnp.concatenate([_dot_split(lf, cum_ref[d]), ig], axis=0)
            hi = rows.astype(BF16)
            lo = (rows - hi.astype(F32)).astype(BF16)
            rows = hi.astype(F32) + lo.astype(F32)
            f_rows, ig = rows[0:nh], rows[nh:2 * nh]
            cols = _tn_dot(hi, sel_ref[...]) + _tn_dot(lo, sel_ref[...])
            for hd in range(nh):
                chains.append(dict(d=d, rev=rev, r0=r0, idx=d * nh + hd, cs=slice(hd * dh, (hd + 1) * dh),
                                   ks=slice(D_MLSTM + hd * dh, D_MLSTM + (hd + 1) * dh),
                                   ig=ig[hd:hd + 1], f_row=f_rows[hd:hd + 1],
                                   f_col=cols[:, hd * dh:(hd + 1) * dh],
                                   i_col=cols[:, (nh + hd) * dh:(nh + hd + 1) * dh]))
        for ch in chains:
            ch["q"] = qk_ref[pl.ds(ch["r0"], t), ch["cs"]]
            ch["k"] = qk_ref[pl.ds(ch["r0"], t), ch["ks"]]
            ch["v"] = v_ref[0, pl.ds(ch["r0"], t), ch["cs"]]
            ch["qk"] = _nt_dot(ch["q"], ch["k"])
            ch["cn0"] = cn_ref[ch["idx"]]
            ch["m0"] = m_ref[ch["idx"]:ch["idx"] + 1, 0:1]
            ch["inter"] = _dot(ch["q"], ch["cn0"].astype(BF16))
        for ch in chains:
            r_row = ch["f_row"] - ch["ig"]
            f_end = ch["f_row"][:, 0:1] if ch["rev"] else ch["f_row"][:, t - 1:t]
            log_d = ch["f_col"] - r_row + neg_ref[ch["d"]]
            log_inter = ch["f_col"] + ch["m0"]
            m_t = jnp.maximum(jnp.max(log_d, axis=1, keepdims=True), log_inter)
            ch["s_qk"] = (ch["qk"] * jnp.exp(log_d - m_t)).astype(BF16)
            ch["e_inter"] = jnp.exp(log_inter - m_t)
            ch["m_t"] = m_t
            m_loc = f_end - jnp.min(r_row, axis=1, keepdims=True)
            e_end = jnp.exp(f_end - (ch["f_col"] - ch["i_col"]) - m_loc)
            ch["w"] = jnp.concatenate([e_end * ch["v"].astype(F32), e_end], axis=1).astype(BF16)
            m_new = jnp.maximum(f_end + ch["m0"], m_loc)
            ch["a"] = jnp.exp(f_end + ch["m0"] - m_new)
            ch["bb"] = jnp.exp(m_loc - m_new)
            ch["m_new"] = m_new
        for ch in chains:
            ch["intra"] = _dot(ch["s_qk"], jnp.concatenate([ch["v"], ones], axis=1))
            ch["d_cn"] = _tn_dot(ch["k"], ch["w"])
        for ch in chains:
            num = ch["intra"][:, 0:dh] + ch["e_inter"] * ch["inter"][:, 0:dh]
            den = ch["intra"][:, dh:2 * dh] + ch["e_inter"] * ch["inter"][:, dh:2 * dh]
            acc_ref[pl.ds(ch["r0"], t), ch["cs"]] += num / jnp.maximum(jnp.abs(den), jnp.exp(-ch["m_t"]))
            cn_ref[ch["idx"]] = ch["a"] * ch["cn0"] + ch["bb"] * ch["d_cn"]
            m_ref[ch["idx"]:ch["idx"] + 1, :] = jnp.broadcast_to(ch["m_new"], (1, LANES))
        return carry

    lax.fori_loop(0, n_chunks, step, 0, unroll=2 if n_chunks % 2 == 0 else 1)

    def finish(c, carry):
        r0 = pl.multiple_of(c * t, t)
        for hd in range(nh):
            cs = slice(hd * dh, (hd + 1) * dh)
            hsum = acc_ref[pl.ds(r0, t), cs]
            y = hsum * lax.rsqrt(jnp.mean(hsum * hsum, axis=-1, keepdims=True) + EPS) * g_ref[:, cs]
            out_ref[0, pl.ds(r0, t), cs] = (y * o_ref[0, pl.ds(r0, t), cs].astype(F32)).astype(BF16)
        return carry

    lax.fori_loop(0, n_chunks, finish, 0)


def _mlstm(proj, gates_t, gate_b, norm_g, conv_w, conv_b, conv_scale, col_qk, col_v, col_o, lc):
    b, lt, _ = proj.shape
    w, t = D_MLSTM, MLSTM_CHUNK
    kern = functools.partial(_mlstm_kernel, n_ctx_chunks=lc // t, n_chunks=lt // t)
    cum, neg, sel = _mlstm_masks()
    return pl.pallas_call(
        kern,
        grid=(b,),
        in_specs=[pl.BlockSpec((1, lt, w), lambda bi: (bi, 0, col_qk // w)),
                  pl.BlockSpec((1, lt, w), lambda bi: (bi, 0, col_qk // w + 1)),
                  pl.BlockSpec((1, lt, w), lambda bi: (bi, 0, col_v // w)),
                  pl.BlockSpec((1, lt, w), lambda bi: (bi, 0, col_o // w)),
                  pl.BlockSpec((1, N_GATES, lt), lambda bi: (bi, 0, 0)),
                  pl.BlockSpec((N_GATES, 1), lambda bi: (0, 0)),
                  pl.BlockSpec((1, w), lambda bi: (0, 0)),
                  pl.BlockSpec(cum.shape, lambda bi: (0, 0, 0)),
                  pl.BlockSpec(neg.shape, lambda bi: (0, 0, 0)),
                  pl.BlockSpec(sel.shape, lambda bi: (0, 0)),
                  pl.BlockSpec((QK_CONV, 2 * w), lambda bi: (0, 0)),
                  pl.BlockSpec((1, 2 * w), lambda bi: (0, 0)),
                  pl.BlockSpec((1, 2 * w), lambda bi: (0, 0))],
        out_specs=pl.BlockSpec((1, lt, w), lambda bi: (bi, 0, 0)),
        out_shape=jax.ShapeDtypeStruct((b, lt, w), BF16),
        scratch_shapes=[pltpu.VMEM((lt, w), F32),
                        pltpu.VMEM((2 * N_MLSTM_HEADS, MLSTM_HEAD, 2 * MLSTM_HEAD), F32),
                        pltpu.VMEM((2 * N_MLSTM_HEADS, LANES), F32),
                        pltpu.VMEM((lt, 2 * w), BF16)],
        compiler_params=_cparams(("parallel",)),
        name="mlstm_mixer",
    )(proj, proj, proj, proj, gates_t, gate_b, norm_g.reshape(1, w), cum, neg, sel,
      conv_w, conv_b.reshape(1, 2 * w), conv_scale)


def _attn_kernel(sc_ref, q_ref, k_ref, v_ref, lam_ref, g_ref, o_ref, vext_ref, *, lc, ctx_tiles):
    i = pl.program_id(2)
    dv = DIFF_V_HEAD
    lp = lam_ref[...]
    lam = (jnp.exp(jnp.sum(lp[0:1] * lp[1:2], axis=1, keepdims=True))
           - jnp.exp(jnp.sum(lp[2:3] * lp[3:4], axis=1, keepdims=True)) + sc_ref[0])
    lane = lax.broadcasted_iota(jnp.int32, (1, dv), 1)

    n_heads = q_ref.shape[2] // dv

    @pl.when(i == 0)
    def _():
        for h in range(n_heads):
            vext_ref[:, 2 * h * dv:(2 * h + 1) * dv] = v_ref[0, :, h * dv:(h + 1) * dv]
            vext_ref[:, (2 * h + 1) * dv:(2 * h + 2) * dv] = jnp.ones((v_ref.shape[1], dv), BF16)

    def attend(nk, kc):
        for h in range(n_heads):
            hs = slice(h * dv, (h + 1) * dv)
            q = q_ref[0, :, hs]
            outs = []
            for c in range(2):
                keep = (lane < DIFF_QK_HEAD) if c == 0 else (lane >= DIFF_QK_HEAD)
                qc = jnp.where(keep, q, jnp.zeros_like(q))
                m_run, acc = None, None
                for k0 in range(0, nk, kc):
                    s = _nt_dot(qc, k_ref[0, k0:k0 + kc, hs])
                    m_j = jnp.max(s, axis=1, keepdims=True)
                    m_new = m_j if m_run is None else jnp.maximum(m_run, m_j)
                    pv = _dot(jnp.exp(s - m_new).astype(BF16), vext_ref[k0:k0 + kc, 2 * h * dv:(2 * h + 2) * dv])
                    acc = pv if m_run is None else acc * jnp.exp(m_run - m_new) + pv
                    m_run = m_new
                outs.append(acc[:, 0:dv] / acc[:, dv:2 * dv])
            o = outs[0] - lam * outs[1]
            y = o * lax.rsqrt(jnp.mean(o * o, axis=-1, keepdims=True) + EPS) * g_ref[...]
            o_ref[0, :, hs] = (y * sc_ref[1]).astype(BF16)

    @pl.when(i < ctx_tiles)
    def _():
        attend(lc, lc)

    @pl.when(i >= ctx_tiles)
    def _():
        nk = k_ref.shape[1]
        attend(nk, ATTN_KEY_CHUNK if nk % ATTN_KEY_CHUNK == 0 else nk)


def _attention(scal, proj, lam_p, subln_g, col_q, col_k, col_v, lc):
    b, lt, _ = proj.shape
    w, dv = D_DIFF, DIFF_V_HEAD
    kern = functools.partial(_attn_kernel, lc=lc, ctx_tiles=lc // ROW_TILE)
    return pl.pallas_call(
        kern,
        grid=(b, 1, lt // ROW_TILE),
        in_specs=[pl.BlockSpec(memory_space=pltpu.SMEM),
                  pl.BlockSpec((1, ROW_TILE, w), lambda bi, hi, i: (bi, i, col_q // w)),
                  pl.BlockSpec((1, lt, w), lambda bi, hi, i: (bi, 0, col_k // w)),
                  pl.BlockSpec((1, lt, w), lambda bi, hi, i: (bi, 0, col_v // w)),
                  pl.BlockSpec((4, DIFF_QK_HEAD), lambda bi, hi, i: (0, 0)),
                  pl.BlockSpec((1, dv), lambda bi, hi, i: (0, 0))],
        out_specs=pl.BlockSpec((1, ROW_TILE, w), lambda bi, hi, i: (bi, i, 0)),
        out_shape=jax.ShapeDtypeStruct((b, lt, w), BF16),
        scratch_shapes=[pltpu.VMEM((lt, 2 * w), BF16)],
        compiler_params=_cparams(("parallel", "parallel", "arbitrary")),
        name="diff_attention",
    )(scal, proj, proj, proj, lam_p, subln_g.reshape(1, dv))


def _merge_kernel(x_ref, mod_ref, y5_ref, sz_ref, hb_ref, mz_ref, oc_ref, dz_ref, gl_ref,
                  gw_ref, gb_ref, wa_ref, wb_ref, wc_ref, wo_ref, o_ref, *, d_model):
    f = lambda r: r[0].astype(F32)
    rows = x_ref.shape[1]
    y = jnp.concatenate([y5_ref[j, :, 0].reshape(rows, LANES) for j in range(S5_SLABS)], axis=1).astype(F32)
    ge = 0.5 * y * (1.0 + jnp.tanh(math.sqrt(2.0 / math.pi) * (y + 0.044715 * (y * y * y))))
    t = _dot(ge.astype(BF16), gw_ref[...]) + gb_ref[...]
    ya = t[:, 0:D_SSM] * jax.nn.sigmoid(t[:, D_SSM:2 * D_SSM])
    pa = _dot((ya * f(sz_ref)).astype(BF16), wa_ref[...])
    pb = _dot((f(hb_ref) * f(mz_ref)).astype(BF16), wb_ref[...])
    pc = _dot((f(oc_ref) * f(dz_ref)).astype(BF16), wc_ref[...])
    gates = f(gl_ref)
    m = (gates[:, 0:d_model] * pa + gates[:, d_model:2 * d_model] * pb
         + gates[:, 2 * d_model:3 * d_model] * pc)
    out = _dot(m.astype(BF16), wo_ref[...])
    o_ref[0] = x_ref[0] + mod_ref[0, :, 2 * d_model:3 * d_model] * out


def _merge(x_all, mods3, y5, hb, oc, proj, glu_w, glu_b, wa, wb, wc, wo, cols, n_batch, lc, skip_ctx):
    b, lt, d = x_all.shape
    ctx_tiles = lc // ROW_TILE
    t0 = ctx_tiles if skip_ctx else 0
    w = D_SSM
    blk = lambda name: cols[name] // w
    blk_rows = ROW_TILE // S5_CHUNK
    tok = lambda j: pl.BlockSpec((1, ROW_TILE, w), lambda bi, i: (bi, i + t0, j))
    full = lambda a: pl.BlockSpec(a.shape, lambda bi, i: (0,) * a.ndim)
    kern = functools.partial(_merge_kernel, d_model=d)
    return pl.pallas_call(
        kern,
        grid=(b, lt // ROW_TILE - t0),
        in_specs=[pl.BlockSpec((1, ROW_TILE, d), lambda bi, i: (bi, i + t0, 0)),
                  pl.BlockSpec((1, 1, 3 * d), lambda bi, i: (jnp.where(i + t0 < ctx_tiles, n_batch, bi), 0, 0)),
                  pl.BlockSpec((S5_SLABS, blk_rows, 1, S5_CHUNK, LANES), lambda bi, i: (0, i + t0, bi, 0, 0)),
                  tok(blk("sz")), tok(0), tok(blk("mz")), tok(0), tok(blk("dz")),
                  pl.BlockSpec((1, ROW_TILE, N_BRANCH * d), lambda bi, i: (bi, i + t0, 0)),
                  full(glu_w), full(glu_b), full(wa), full(wb), full(wc), full(wo)],
        out_specs=pl.BlockSpec((1, ROW_TILE, d), lambda bi, i: (bi, i, 0)),
        out_shape=jax.ShapeDtypeStruct((b, lt - t0 * ROW_TILE, d), F32),
        compiler_params=_cparams(("parallel", "parallel")),
        name="merge_outproj",
    )(x_all, mods3, y5, proj, hb, proj, oc, proj, proj, glu_w, glu_b, wa, wb, wc, wo)


def _rope_tables(lc, l):
    half = DIFF_QK_HEAD // 2
    rows = l // GRID_W
    row = jnp.repeat(jnp.arange(rows), GRID_W).astype(F32)
    col = jnp.tile(jnp.arange(GRID_W), rows).astype(F32)
    inv = jnp.power(ROPE_BASE, -jnp.arange(0, half, 2, dtype=F32) / half)
    ang_r, ang_c = row[:, None] * inv, col[:, None] * inv
    cos64 = jnp.concatenate([jnp.cos(ang_r), jnp.cos(ang_r), jnp.cos(ang_c), jnp.cos(ang_c)], axis=1)
    sin64 = jnp.concatenate([-jnp.sin(ang_r), jnp.sin(ang_r), -jnp.sin(ang_c), jnp.sin(ang_c)], axis=1)
    cos_t = jnp.concatenate([jnp.ones((lc, DIFF_QK_HEAD), F32), cos64], axis=0)
    sin_t = jnp.concatenate([jnp.zeros((lc, DIFF_QK_HEAD), F32), sin64], axis=0)
    return jnp.tile(cos_t, (1, 2)), jnp.tile(sin_t, (1, 2))


def kernel(x, c, ctx, c_ctx, norm_g, ada_w, ada_b, w_in, ssm_lam_re, ssm_lam_im, ssm_log_step, ssm_b_re, ssm_b_im, ssm_c_re, ssm_c_im, ssm_d, ssm_glu_w, ssm_glu_b, w_ssm_out, ml_conv_w, ml_conv_b, ml_gate_b, ml_norm_g, w_ml_out, da_qnorm_g, da_knorm_g, da_lambda, da_subln_g, w_da_out, w_out):
    n_batch, l, d = x.shape
    lc = ctx.shape[1]
    lt = lc + l
    depth = w_in.shape[0]
    cols = _cols(d)
    t5 = S5_CHUNK
    n_rows = (lt // t5) * n_batch
    assert lc % ROW_TILE == 0 and l % ROW_TILE == 0 and lc % MLSTM_CHUNK == 0 and n_batch % SUBLANES == 0
    assert MLSTM_CHUNK == MLSTM_HEAD == LANES

    x_all = jnp.concatenate([ctx, x], axis=1)

    mod_rows = -(-(n_batch + 1) // SUBLANES) * SUBLANES
    cvec = jnp.zeros((mod_rows, d), F32).at[:n_batch].set(c).at[n_batch].set(c_ctx)
    mods = _modulation(cvec, ada_w, ada_b)

    g0 = 2 * D_SSM + 5 * D_MLSTM
    g1 = g0 + N_GATES
    m0 = g1 + 4 * D_DIFF
    w_p = jnp.concatenate([w_in[:, :, m0:], w_in[:, :, D_SSM:g0], w_in[:, :, g1:m0]], axis=2).astype(BF16)
    w_su = w_in[:, :, :D_SSM].astype(BF16)
    w_gt = jnp.swapaxes(w_in[:, :, g0:g1], 1, 2).astype(BF16)

    cos_t, sin_t = _rope_tables(lc, l)
    ones_bd = jnp.kron(jnp.eye(2 * LANES // DIFF_QK_HEAD, dtype=F32),
                       jnp.ones((DIFF_QK_HEAD, DIFF_QK_HEAD), F32)).astype(BF16)
    qk_scale = jnp.concatenate([jnp.ones((1, D_MLSTM), F32), jnp.full((1, D_MLSTM), MLSTM_HEAD ** -0.5, F32)], 1)
    perm_in, perm_out = _s5_lane_permutations()
    s5_m1, s5_m2, s5_dec = _s5_block_operators(ssm_lam_re, ssm_lam_im, ssm_log_step, ssm_b_re, ssm_b_im,
                                               ssm_c_re, ssm_c_im, ssm_d)

    for li in range(depth):
        last = li == depth - 1
        mods3 = mods[li].reshape(mod_rows, 1, 3 * d)
        qg = jnp.tile(da_qnorm_g[li].astype(F32), D_DIFF // DIFF_QK_HEAD).reshape(1, D_DIFF)
        kg = jnp.tile(da_knorm_g[li].astype(F32), D_DIFF // DIFF_QK_HEAD).reshape(1, D_DIFF)
        proj, su, gates_t = _inproj(x_all, mods3, norm_g[li], w_p[li], w_su[li], w_gt[li], cos_t, sin_t, qg, kg,
                                    ones_bd, cols, n_batch, lc)

        xb = _s5_gather(su.reshape(S5_SLABS, n_rows, S5_WIDE), perm_in)
        yb = _s5_mixer(xb, s5_m1[li], s5_m2[li], s5_dec[li], n_batch, lc // t5)
        y5 = _s5_scatter(yb, perm_out).reshape(su.shape)

        hb = _mlstm(proj, gates_t, ml_gate_b[li].astype(F32).reshape(N_GATES, 1), ml_norm_g[li],
                    ml_conv_w[li], ml_conv_b[li], qk_scale, cols["mqk"], cols["mv"], cols["mo"], lc)

        lam_init = 0.8 - 0.6 * math.exp(-0.3 * li)
        scal = jnp.array([lam_init, 1.0 - lam_init], F32)
        oc = _attention(scal, proj, da_lambda[li].astype(F32), da_subln_g[li].astype(F32),
                        cols["dq"], cols["dk"], cols["dv"], lc)

        x_all = _merge(x_all, mods3, y5, hb, oc, proj, ssm_glu_w[li].astype(BF16),
                       ssm_glu_b[li].astype(F32).reshape(1, -1), w_ssm_out[li].astype(BF16),
                       w_ml_out[li].astype(BF16), w_da_out[li].astype(BF16), w_out[li].astype(BF16),
                       cols, n_batch, lc, skip_ctx=last)
    return x_all
```
